```python
import jax, jax.numpy as jnp
from jax import lax
import numpy as np

D_MODEL = 1024
BATCH = 2
SEQ = 8192
DEPTH = 1
DEC_BATCH = 2
DEC_SEQ = 16384
PAST_LEN = 128

HEAD_DIM = 64
A_PAIRS = ((128, 1), (512, 4), (2048, 16))
A_SLOTS = 4
A_HEADS = A_SLOTS * len(A_PAIRS)
A_OUT = A_SLOTS * HEAD_DIM
B_Q_HEADS = 8
B_KV_HEADS = 2
B_HALF_WINDOW = 128
B_OUT = B_Q_HEADS * HEAD_DIM
D_FF = 2816
A_W = A_HEADS * HEAD_DIM
B_KV = B_KV_HEADS * HEAD_DIM
IN_SPLITS = (A_W, 2 * A_W, 3 * A_W, 3 * A_W + B_OUT, 3 * A_W + B_OUT + B_KV)
D_IN = 3 * A_W + B_OUT + 2 * B_KV
NORM_EPS = 1e-6
MASK_VALUE = -1e30

kernel_name = "hybrid_dilated_window_gqa_macaron_encoder"


def rms_norm(x, g):
    xf = x.astype(jnp.float32)
    y = xf * lax.rsqrt(jnp.mean(xf * xf, axis=-1, keepdims=True) + NORM_EPS)
    return (y * g.astype(jnp.float32)).astype(x.dtype)


def swiglu(x, w_gate, w_up, w_down):
    return (jax.nn.silu(x @ w_gate) * (x @ w_up)) @ w_down


def alibi_slopes(n):
    return 2.0 ** (-8.0 * jnp.arange(1, n + 1, dtype=jnp.float32) / n)


def banded_attention(q, k, v, half, dist_scale, slopes, sink=None):
    B, L, H, Dh = q.shape
    Hk = k.shape[2]
    G = H // Hk
    blk = half
    nb = -(-L // blk)
    Lp = nb * blk
    pad = Lp - L
    q = jnp.pad(q, ((0, 0), (0, pad), (0, 0), (0, 0)))
    kv_pad = ((0, 0), (blk, pad + blk), (0, 0), (0, 0))
    k = jnp.pad(k, kv_pad)
    v = jnp.pad(v, kv_pad)

    def band(t):
        return jnp.concatenate(
            [t[:, j * blk: j * blk + Lp].reshape(B, nb, blk, Hk, Dh) for j in range(3)], axis=2)

    kb, vb = band(k), band(v)
    qb = q.reshape(B, nb, blk, Hk, G, Dh)
    s = jnp.einsum('bnqhgd,bnkhd->bnhgqk', qb, kb).astype(jnp.float32) * (Dh ** -0.5)
    rel = jnp.arange(3 * blk)[None, :] - blk - jnp.arange(blk)[:, None]
    key_pos = jnp.arange(nb)[:, None] * blk - blk + jnp.arange(3 * blk)[None, :]
    valid = (jnp.abs(rel)[None] <= half) & ((key_pos >= 0) & (key_pos < L))[:, None, :]
    dist = (jnp.abs(rel) * dist_scale).astype(jnp.float32)
    s = s - slopes.reshape(Hk, G)[:, :, None, None] * dist
    s = jnp.where(valid[None, :, None, None], s, MASK_VALUE)
    lse = jax.nn.logsumexp(s, axis=-1)
    if sink is not None:
        lse = jnp.logaddexp(lse, sink.astype(jnp.float32).reshape(Hk, G)[:, :, None])
    p = jnp.exp(s - lse[..., None]).astype(v.dtype)
    o = jnp.einsum('bnhgqk,bnkhd->bnqhgd', p, vb).reshape(B, Lp, H, Dh)[:, :L]
    lse = lse.transpose(0, 1, 4, 2, 3).reshape(B, Lp, H)[:, :L]
    return o, lse


def to_residues(t, d):
    B, T, h, Dh = t.shape
    return t.reshape(B, T // d, d, h, Dh).transpose(0, 2, 1, 3, 4).reshape(B * d, T // d, h, Dh)


def from_residues(t, B, d):
    L = t.shape[1]
    rest = t.shape[2:]
    return t.reshape((B, d, L) + rest).swapaxes(1, 2).reshape((B, L * d) + rest)


def dilated_attention(q, k, v):
    B = q.shape[0]
    slopes = alibi_slopes(A_HEADS).reshape(len(A_PAIRS), A_SLOTS)
    outs, lses = [], []
    for gi, (w, d) in enumerate(A_PAIRS):
        sl = slice(gi * A_SLOTS, (gi + 1) * A_SLOTS)
        o, lse = banded_attention(to_residues(q[:, :, sl], d), to_residues(k[:, :, sl], d),
                                  to_residues(v[:, :, sl], d), w // (2 * d), d, slopes[gi])
        outs.append(from_residues(o, B, d))
        lses.append(from_residues(lse, B, d))
    outs = jnp.stack(outs, axis=0)
    alpha = jax.nn.softmax(jnp.stack(lses, axis=0), axis=0)
    return jnp.sum(alpha[..., None] * outs, axis=0).astype(q.dtype)


def encoder_layer(x, ffn1_norm, ffn1_w_gate, ffn1_w_up, ffn1_w_down, mix_norm, w_in,
                  a_q_norm, a_k_norm, b_q_norm, b_k_norm, b_sink, w_proj_a, w_proj_b,
                  w_gate, b_gate, w_out, ffn2_norm, ffn2_w_gate, ffn2_w_up, ffn2_w_down, final_norm):
    B, T, _ = x.shape
    h = x + 0.5 * swiglu(rms_norm(x, ffn1_norm), ffn1_w_gate, ffn1_w_up, ffn1_w_down)
    u = rms_norm(h, mix_norm)
    qa, ka, va, qb, kb, vb = jnp.split(u @ w_in, IN_SPLITS, axis=-1)
    qa = rms_norm(qa.reshape(B, T, A_HEADS, HEAD_DIM), a_q_norm)
    ka = rms_norm(ka.reshape(B, T, A_HEADS, HEAD_DIM), a_k_norm)
    va = va.reshape(B, T, A_HEADS, HEAD_DIM)
    qb = rms_norm(qb.reshape(B, T, B_Q_HEADS, HEAD_DIM), b_q_norm)
    kb = rms_norm(kb.reshape(B, T, B_KV_HEADS, HEAD_DIM), b_k_norm)
    vb = vb.reshape(B, T, B_KV_HEADS, HEAD_DIM)
    ya = dilated_attention(qa, ka, va).reshape(B, T, A_OUT)
    yb, _ = banded_attention(qb, kb, vb, B_HALF_WINDOW, 1, alibi_slopes(B_Q_HEADS), b_sink)
    yb = yb.reshape(B, T, B_OUT)
    ga, gb = jnp.split(jax.nn.sigmoid(u @ w_gate + b_gate), 2, axis=-1)
    m = ga * (ya @ w_proj_a) + gb * (yb @ w_proj_b)
    h = h + m @ w_out
    h = h + 0.5 * swiglu(rms_norm(h, ffn2_norm), ffn2_w_gate, ffn2_w_up, ffn2_w_down)
    return rms_norm(h, final_norm)


def setup_inputs(seed: int = 0) -> dict:
    key = jax.random.key(seed)
    ks = jax.random.split(key, 24)
    f32 = jnp.float32

    def w(k, shape, fan_in):
        return jax.random.normal(k, (DEPTH,) + shape, f32) * (fan_in ** -0.5)

    def gain(k, n):
        return 1.0 + 0.02 * jax.random.normal(k, (DEPTH, n), f32)

    return {
        "x_prompt": jax.random.normal(ks[0], (BATCH, SEQ, D_MODEL), f32),
        "x_sample": jax.random.normal(ks[1], (DEC_BATCH, DEC_SEQ, D_MODEL), f32),
        "ffn1_norm": gain(ks[2], D_MODEL),
        "ffn1_w_gate": w(ks[3], (D_MODEL, D_FF), D_MODEL),
        "ffn1_w_up": w(ks[4], (D_MODEL, D_FF), D_MODEL),
        "ffn1_w_down": w(ks[5], (D_FF, D_MODEL), D_FF),
        "mix_norm": gain(ks[6], D_MODEL),
        "w_in": w(ks[7], (D_MODEL, D_IN), D_MODEL),
        "a_q_norm": gain(ks[8], HEAD_DIM),
        "a_k_norm": gain(ks[9], HEAD_DIM),
        "b_q_norm": gain(ks[10], HEAD_DIM),
        "b_k_norm": gain(ks[11], HEAD_DIM),
        "b_sink": 0.5 * jax.random.normal(ks[12], (DEPTH, B_Q_HEADS), f32),
        "w_proj_a": w(ks[13], (A_OUT, D_MODEL), A_OUT),
        "w_proj_b": w(ks[14], (B_OUT, D_MODEL), B_OUT),
        "w_gate": w(ks[15], (D_MODEL, 2 * D_MODEL), D_MODEL),
        "b_gate": 0.02 * jax.random.normal(ks[16], (DEPTH, 2 * D_MODEL), f32),
        "w_out": w(ks[17], (D_MODEL, D_MODEL), D_MODEL),
        "ffn2_norm": gain(ks[18], D_MODEL),
        "ffn2_w_gate": w(ks[19], (D_MODEL, D_FF), D_MODEL),
        "ffn2_w_up": w(ks[20], (D_MODEL, D_FF), D_MODEL),
        "ffn2_w_down": w(ks[21], (D_FF, D_MODEL), D_FF),
        "final_norm": gain(ks[22], D_MODEL),
    }


def reference(x_prompt, x_sample, ffn1_norm, ffn1_w_gate, ffn1_w_up, ffn1_w_down, mix_norm, w_in,
              a_q_norm, a_k_norm, b_q_norm, b_k_norm, b_sink, w_proj_a, w_proj_b, w_gate, b_gate,
              w_out, ffn2_norm, ffn2_w_gate, ffn2_w_up, ffn2_w_down, final_norm):
    def encoder(x):
        for l in range(DEPTH):
            x = encoder_layer(x, ffn1_norm[l], ffn1_w_gate[l], ffn1_w_up[l], ffn1_w_down[l],
                              mix_norm[l], w_in[l], a_q_norm[l], a_k_norm[l], b_q_norm[l],
                              b_k_norm[l], b_sink[l], w_proj_a[l], w_proj_b[l], w_gate[l],
                              b_gate[l], w_out[l], ffn2_norm[l], ffn2_w_gate[l], ffn2_w_up[l],
                              ffn2_w_down[l], final_norm[l])
        return x

    y_prompt = encoder(x_prompt)
    y_sample = encoder(x_sample)
    return (y_prompt, y_sample)
```

```python
import functools

import numpy as np
import jax
import jax.numpy as jnp
from jax import lax
from jax.experimental import pallas as pl
from jax.experimental.pallas import tpu as pltpu

D_MODEL = 1024
D_FF = 2816
HEAD_DIM = 64
A_PAIRS = ((128, 1), (512, 4), (2048, 16))
A_SLOTS = 4
A_HEADS = A_SLOTS * len(A_PAIRS)
A_OUT = A_SLOTS * HEAD_DIM
A_W = A_HEADS * HEAD_DIM
B_Q_HEADS = 8
B_KV_HEADS = 2
B_HALF_WINDOW = 128
B_OUT = B_Q_HEADS * HEAD_DIM
B_KV = B_KV_HEADS * HEAD_DIM
NORM_EPS = 1e-6
MASK_VALUE = -1e30

N_NORM = 2 * A_W + B_OUT + B_KV
N_VAL = A_W + B_KV
N_HEADS_NORM = N_NORM // HEAD_DIM
HEAD_LANES = 128
QKVA_W = 3 * A_W
QKVB_W = B_OUT + 2 * B_KV

FF_CHUNK = 256
N_FF_CHUNKS = D_FF // FF_CHUNK
TOKEN_TILE = 256
ATTN_BQ = 128
A_HALF = 64
V7X_VMEM_LIMIT_BYTES = 56 * 1024 * 1024

_F32 = jnp.float32
_BF16 = jnp.bfloat16


def _rms(x, g):
    ms = jnp.mean(x * x, axis=-1, keepdims=True)
    return x * lax.rsqrt(ms + NORM_EPS) * g


def _swiglu_into(acc_ref, xn, wg_ref, wu_ref, wd_ref):
    acc_ref[...] = jnp.zeros_like(acc_ref)

    def body(c, carry):
        g = jnp.dot(xn, wg_ref[c], preferred_element_type=_F32)
        u = jnp.dot(xn, wu_ref[c], preferred_element_type=_F32)
        a = (g * jax.nn.sigmoid(g) * u).astype(_BF16)
        acc_ref[...] += jnp.dot(a, wd_ref[c], preferred_element_type=_F32)
        return carry

    lax.fori_loop(0, N_FF_CHUNKS, body, 0)


def _ffn_proj_kernel(x_ref, g1_ref, wg_ref, wu_ref, wd_ref, gmix_ref, win_ref, e_ref, et_ref,
                     gain_ref, h_ref, qkva_ref, qkvb_ref, acc_ref):
    x = x_ref[...]
    xn = _rms(x, g1_ref[...]).astype(_BF16)
    _swiglu_into(acc_ref, xn, wg_ref, wu_ref, wd_ref)
    h = x + 0.5 * acc_ref[...]
    h_ref[...] = h
    u = _rms(h, gmix_ref[...]).astype(_BF16)
    qkv = jnp.dot(u, win_ref[...], preferred_element_type=_F32)
    nrm = qkv[:, :N_NORM]
    ssq = jnp.dot((nrm * nrm).astype(_BF16), e_ref[...], preferred_element_type=_F32)
    inv = lax.rsqrt(ssq * (1.0 / HEAD_DIM) + NORM_EPS)
    inv_hi = inv.astype(_BF16)
    inv_lo = (inv - inv_hi.astype(_F32)).astype(_BF16)
    inv_b = jnp.dot(jnp.concatenate([inv_hi, inv_lo], axis=1), et_ref[...],
                    preferred_element_type=_F32)
    nq = (nrm * inv_b * gain_ref[...]).astype(_BF16)
    val = qkv[:, N_NORM:].astype(_BF16)
    qkva_ref[:, :2 * A_W] = nq[:, :2 * A_W]
    qkva_ref[:, 2 * A_W:] = val[:, :A_W]
    qkvb_ref[:, :B_OUT + B_KV] = nq[:, 2 * A_W:]
    qkvb_ref[:, B_OUT + B_KV:] = val[:, A_W:]


def _attn_a_kernel(q_ref, kp_ref, ko_ref, kn_ref, vp_ref, vo_ref, vn_ref, o_ref, lse_ref, bias_ref,
                   *, dilation, slopes, n_blocks):
    n = pl.program_id(2)
    bq = ATTN_BQ
    bk = bq + 2 * A_HALF

    @pl.when((pl.program_id(0) == 0) & (pl.program_id(1) == 0) & (n == 0))
    def _():
        row = lax.broadcasted_iota(jnp.int32, (A_SLOTS * bq, bk), 0)
        col = lax.broadcasted_iota(jnp.int32, (A_SLOTS * bq, bk), 1)
        rel = jnp.abs(col - A_HALF - (row % bq))
        head = row // bq
        slope = jnp.full((A_SLOTS * bq, bk), slopes[0], _F32)
        for h in range(1, A_SLOTS):
            slope = jnp.where(head == h, slopes[h], slope)
        dist = (rel * dilation).astype(_F32)
        bias_ref[...] = jnp.where(rel <= A_HALF, -slope * dist, MASK_VALUE)

    q = q_ref[...]
    lane_head = lax.broadcasted_iota(jnp.int32, (bq, A_OUT), 1) // HEAD_DIM
    zero = jnp.zeros_like(q)
    qs = jnp.concatenate([jnp.where(lane_head == h, q, zero) for h in range(A_SLOTS)], axis=0)
    k = jnp.concatenate([kp_ref[...], ko_ref[...], kn_ref[...]], axis=0)
    v = jnp.concatenate([vp_ref[...], vo_ref[...], vn_ref[...]], axis=0)
    s = lax.dot_general(qs, k, (((1,), (1,)), ((), ())), preferred_element_type=_F32)
    s = s + bias_ref[...]
    col = lax.broadcasted_iota(jnp.int32, (1, bk), 1)
    lo = jnp.where(n == 0, A_HALF, 0)
    hi = jnp.where(n == n_blocks - 1, bq + A_HALF, bk)
    s = jnp.where((col >= lo) & (col < hi), s, MASK_VALUE)
    m = jnp.max(s, axis=1, keepdims=True)
    p = jnp.exp(s - m)
    l = jnp.sum(p, axis=1, keepdims=True)
    pv = jnp.dot(p.astype(_BF16), v, preferred_element_type=_F32)
    on = pv * (1.0 / l)
    lse = m + jnp.log(l)
    o = on[:bq]
    lse_t = jnp.broadcast_to(lse[:bq], (bq, A_OUT))
    for h in range(1, A_SLOTS):
        sel = lane_head == h
        o = jnp.where(sel, on[h * bq:(h + 1) * bq], o)
        lse_t = jnp.where(sel, jnp.broadcast_to(lse[h * bq:(h + 1) * bq], (bq, A_OUT)), lse_t)
    o_ref[...] = o.astype(o_ref.dtype)
    lse_ref[...] = lse_t


def _attn_b_kernel(q_ref, kp_ref, ko_ref, kn_ref, vp_ref, vo_ref, vn_ref, sink_ref, o_ref, bias_ref,
                   *, slopes, n_blocks):
    n = pl.program_id(1)
    bq = ATTN_BQ
    bk = 3 * bq
    rows = B_Q_HEADS * bq

    @pl.when((pl.program_id(0) == 0) & (n == 0))
    def _():
        row = lax.broadcasted_iota(jnp.int32, (rows, bk), 0)
        col = lax.broadcasted_iota(jnp.int32, (rows, bk), 1)
        rel = jnp.abs(col - bq - (row % bq))
        head = row // bq
        slope = jnp.full((rows, bk), slopes[0], _F32)
        for h in range(1, B_Q_HEADS):
            slope = jnp.where(head == h, slopes[h], slope)
        bias_ref[...] = jnp.where(rel <= B_HALF_WINDOW, -slope * rel.astype(_F32), MASK_VALUE)

    lane_kv = lax.broadcasted_iota(jnp.int32, (bq, B_KV), 1) // HEAD_DIM
    parts = []
    for j in range(B_KV_HEADS):
        for c in range(B_Q_HEADS // B_KV_HEADS):
            chunk = q_ref[:, c * B_KV:(c + 1) * B_KV]
            parts.append(jnp.where(lane_kv == j, chunk, jnp.zeros_like(chunk)))
    qs = jnp.concatenate(parts, axis=0)
    k = jnp.concatenate([kp_ref[...], ko_ref[...], kn_ref[...]], axis=0)
    v = jnp.concatenate([vp_ref[...], vo_ref[...], vn_ref[...]], axis=0)
    s = lax.dot_general(qs, k, (((1,), (1,)), ((), ())), preferred_element_type=_F32)
    s = s + bias_ref[...]
    col = lax.broadcasted_iota(jnp.int32, (1, bk), 1)
    lo = jnp.where(n == 0, bq, 0)
    hi = jnp.where(n == n_blocks - 1, 2 * bq, bk)
    s = jnp.where((col >= lo) & (col < hi), s, MASK_VALUE)
    sink = sink_ref[...]
    m = jnp.maximum(jnp.max(s, axis=1, keepdims=True), sink)
    p = jnp.exp(s - m)
    l = jnp.sum(p, axis=1, keepdims=True) + jnp.exp(sink - m)
    pv = jnp.dot(p.astype(_BF16), v, preferred_element_type=_F32)
    on = pv * (1.0 / l)
    group = B_Q_HEADS // B_KV_HEADS
    for c in range(group):
        o_ref[:, c * B_KV:(c + 1) * B_KV] = jnp.where(
            lane_kv == 0, on[c * bq:(c + 1) * bq], on[(group + c) * bq:(group + c + 1) * bq]
        ).astype(o_ref.dtype)


def _mix_ffn_kernel(h_ref, oa0_ref, la0_ref, oa1_ref, la1_ref, oa2_ref, la2_ref, yb_ref,
                    gmix_ref, wgate_ref, bgate_ref, wpa_ref, wpb_ref, wout_ref,
                    g2_ref, wg_ref, wu_ref, wd_ref, gfin_ref, y_ref, acc_ref):
    h = h_ref[...]
    u = _rms(h, gmix_ref[...]).astype(_BF16)
    gates = jax.nn.sigmoid(jnp.dot(u, wgate_ref[...], preferred_element_type=_F32) + bgate_ref[...])
    la0, la1, la2 = la0_ref[...], la1_ref[...], la2_ref[...]
    mx = jnp.maximum(jnp.maximum(la0, la1), la2)
    e0, e1, e2 = jnp.exp(la0 - mx), jnp.exp(la1 - mx), jnp.exp(la2 - mx)
    ya = (e0 * oa0_ref[...].astype(_F32) + e1 * oa1_ref[...].astype(_F32)
          + e2 * oa2_ref[...].astype(_F32)) / (e0 + e1 + e2)
    pa = jnp.dot(ya.astype(_BF16), wpa_ref[...], preferred_element_type=_F32)
    pb = jnp.dot(yb_ref[...], wpb_ref[...], preferred_element_type=_F32)
    mixed = gates[:, :D_MODEL] * pa + gates[:, D_MODEL:] * pb
    h = h + jnp.dot(mixed.astype(_BF16), wout_ref[...], preferred_element_type=_F32)
    hn = _rms(h, g2_ref[...]).astype(_BF16)
    _swiglu_into(acc_ref, hn, wg_ref, wu_ref, wd_ref)
    h = h + 0.5 * acc_ref[...]
    y_ref[...] = _rms(h, gfin_ref[...])


def _resident(shape):
    nd = len(shape)
    return pl.BlockSpec(shape, lambda *_: (0,) * nd, pipeline_mode=pl.Buffered(1))


def _ffn_proj(x, p):
    n_tok = x.shape[0]
    tm = TOKEN_TILE
    row = lambda w: pl.BlockSpec((tm, w), lambda i: (i, 0))
    consts = (p["g1"], p["wg1"], p["wu1"], p["wd1"], p["gmix"], p["win"], p["e"], p["et"], p["gain"])
    return pl.pallas_call(
        _ffn_proj_kernel,
        grid=(n_tok // tm,),
        in_specs=[row(D_MODEL)] + [_resident(c.shape) for c in consts],
        out_specs=[row(D_MODEL), row(QKVA_W), row(QKVB_W)],
        out_shape=[jax.ShapeDtypeStruct((n_tok, D_MODEL), _F32),
                   jax.ShapeDtypeStruct((n_tok, QKVA_W), _BF16),
                   jax.ShapeDtypeStruct((n_tok, QKVB_W), _BF16)],
        scratch_shapes=[pltpu.VMEM((tm, D_MODEL), _F32)],
        compiler_params=pltpu.CompilerParams(
            dimension_semantics=("arbitrary",), vmem_limit_bytes=V7X_VMEM_LIMIT_BYTES),
        name="ffn_proj",
    )(x, *consts)


def _attn_a(qkva, batch, seq, group):
    _, d = A_PAIRS[group]
    length = seq // d
    bq = ATTN_BQ
    nq = length // bq
    halo = bq // A_HALF
    slopes = [float(2.0 ** (-8.0 * (group * A_SLOTS + h + 1) / A_HEADS)) for h in range(A_SLOTS)]
    arr = qkva.reshape(batch, length, d * QKVA_W)
    ncol = QKVA_W // A_OUT

    def own(which):
        return pl.BlockSpec((None, bq, A_OUT), lambda b, r, n: (b, n, r * ncol + which * 3 + group))

    def prev(which):
        return pl.BlockSpec((None, A_HALF, A_OUT),
                            lambda b, r, n: (b, jnp.maximum(n * halo - 1, 0), r * ncol + which * 3 + group))

    def nxt(which):
        return pl.BlockSpec((None, A_HALF, A_OUT),
                            lambda b, r, n: (b, jnp.minimum((n + 1) * halo, nq * halo - 1),
                                             r * ncol + which * 3 + group))

    out_spec = pl.BlockSpec((None, bq, A_OUT), lambda b, r, n: (b, n, r))
    o, lse = pl.pallas_call(
        functools.partial(_attn_a_kernel, dilation=d, slopes=slopes, n_blocks=nq),
        grid=(batch, d, nq),
        in_specs=[own(0), prev(1), own(1), nxt(1), prev(2), own(2), nxt(2)],
        out_specs=[out_spec, out_spec],
        out_shape=[jax.ShapeDtypeStruct((batch, length, d * A_OUT), _BF16),
                   jax.ShapeDtypeStruct((batch, length, d * A_OUT), _F32)],
        scratch_shapes=[pltpu.VMEM((A_SLOTS * bq, bq + 2 * A_HALF), _F32)],
        compiler_params=pltpu.CompilerParams(dimension_semantics=("arbitrary",) * 3),
        name=f"attn_a{group}",
    )(arr, arr, arr, arr, arr, arr, arr)
    return o.reshape(batch * seq, A_OUT), lse.reshape(batch * seq, A_OUT)


def _attn_b(qkvb, sink_col, batch, seq):
    bq = ATTN_BQ
    nq = seq // bq
    slopes = [float(2.0 ** (-8.0 * (h + 1) / B_Q_HEADS)) for h in range(B_Q_HEADS)]
    arr = qkvb.reshape(batch, seq, QKVB_W)
    k_col = B_OUT // B_KV

    def kv(col, shift):
        return pl.BlockSpec((None, bq, B_KV),
                            lambda b, n: (b, jnp.clip(n + shift, 0, nq - 1), col))

    yb = pl.pallas_call(
        functools.partial(_attn_b_kernel, slopes=slopes, n_blocks=nq),
        grid=(batch, nq),
        in_specs=[pl.BlockSpec((None, bq, B_OUT), lambda b, n: (b, n, 0)),
                  kv(k_col, -1), kv(k_col, 0), kv(k_col, 1),
                  kv(k_col + 1, -1), kv(k_col + 1, 0), kv(k_col + 1, 1),
                  pl.BlockSpec((B_Q_HEADS * bq, 1), lambda b, n: (0, 0))],
        out_specs=pl.BlockSpec((None, bq, B_OUT), lambda b, n: (b, n, 0)),
        out_shape=jax.ShapeDtypeStruct((batch, seq, B_OUT), _BF16),
        scratch_shapes=[pltpu.VMEM((B_Q_HEADS * bq, 3 * bq), _F32)],
        compiler_params=pltpu.CompilerParams(dimension_semantics=("arbitrary",) * 2),
        name="attn_b",
    )(arr, arr, arr, arr, arr, arr, arr, sink_col)
    return yb.reshape(batch * seq, B_OUT)


def _mix_ffn(h, oa, la, yb, p):
    n_tok = h.shape[0]
    tm = TOKEN_TILE
    row = lambda w: pl.BlockSpec((tm, w), lambda i: (i, 0))
    consts = (p["gmix"], p["wgate"], p["bgate"], p["wpa"], p["wpb"], p["wout"],
              p["g2"], p["wg2"], p["wu2"], p["wd2"], p["gfin"])
    return pl.pallas_call(
        _mix_ffn_kernel,
        grid=(n_tok // tm,),
        in_specs=[row(D_MODEL)] + [row(A_OUT)] * 6 + [row(B_OUT)] + [_resident(c.shape) for c in consts],
        out_specs=row(D_MODEL),
        out_shape=jax.ShapeDtypeStruct((n_tok, D_MODEL), _F32),
        scratch_shapes=[pltpu.VMEM((tm, D_MODEL), _F32)],
        compiler_params=pltpu.CompilerParams(
            dimension_semantics=("arbitrary",), vmem_limit_bytes=V7X_VMEM_LIMIT_BYTES),
        name="mix_ffn",
    )(h, oa[0], la[0], oa[1], la[1], oa[2], la[2], yb, *consts)


def _b_head_perm():
    group = B_Q_HEADS // B_KV_HEADS
    idx = [(j * group + c) * HEAD_DIM + e
           for c in range(group) for j in range(B_KV_HEADS) for e in range(HEAD_DIM)]
    return np.asarray(idx, dtype=np.int32)


def _prepare_params(ffn1_norm, ffn1_w_gate, ffn1_w_up, ffn1_w_down, mix_norm, w_in, a_q_norm,
                    a_k_norm, b_q_norm, b_k_norm, b_sink, w_proj_a, w_proj_b, w_gate, b_gate, w_out,
                    ffn2_norm, ffn2_w_gate, ffn2_w_up, ffn2_w_down, final_norm):
    def vec(g):
        return g.reshape(1, -1).astype(_F32)

    def chunk_cols(w):
        return w.astype(_BF16).reshape(D_MODEL, N_FF_CHUNKS, FF_CHUNK).transpose(1, 0, 2)

    def chunk_rows(w):
        return w.astype(_BF16).reshape(N_FF_CHUNKS, FF_CHUNK, D_MODEL)

    perm = _b_head_perm()
    qa, ka, va, qb, kb, vb = jnp.split(w_in, np.cumsum([A_W, A_W, A_W, B_OUT, B_KV])[:5].tolist(), axis=1)
    win = jnp.concatenate([qa, ka, qb[:, perm], kb, va, vb], axis=1).astype(_BF16)
    scale = HEAD_DIM ** -0.5
    gain = jnp.concatenate([jnp.tile(a_q_norm, A_HEADS) * scale, jnp.tile(a_k_norm, A_HEADS),
                            jnp.tile(b_q_norm, B_Q_HEADS) * scale, jnp.tile(b_k_norm, B_KV_HEADS)])
    head_of_col = np.arange(N_NORM) // HEAD_DIM
    member = (head_of_col[:, None] == np.arange(HEAD_LANES)[None, :])
    e = jnp.asarray(member, dtype=_BF16)
    et = jnp.asarray(np.concatenate([member.T, member.T], axis=0), dtype=_BF16)
    sink_col = jnp.repeat(b_sink.astype(_F32), ATTN_BQ).reshape(B_Q_HEADS * ATTN_BQ, 1)
    return dict(
        g1=vec(ffn1_norm), wg1=chunk_cols(ffn1_w_gate), wu1=chunk_cols(ffn1_w_up), wd1=chunk_rows(ffn1_w_down),
        gmix=vec(mix_norm), win=win, e=e, et=et, gain=vec(gain), sink_col=sink_col,
        wgate=w_gate.astype(_BF16), bgate=vec(b_gate), wpa=w_proj_a.astype(_BF16),
        wpb=w_proj_b[perm, :].astype(_BF16), wout=w_out.astype(_BF16),
        g2=vec(ffn2_norm), wg2=chunk_cols(ffn2_w_gate), wu2=chunk_cols(ffn2_w_up), wd2=chunk_rows(ffn2_w_down),
        gfin=vec(final_norm),
    )


def _encoder(x, p):
    batch, seq, _ = x.shape
    h, qkva, qkvb = _ffn_proj(x.reshape(batch * seq, D_MODEL), p)
    oa, la = zip(*[_attn_a(qkva, batch, seq, g) for g in range(len(A_PAIRS))])
    yb = _attn_b(qkvb, p["sink_col"], batch, seq)
    y = _mix_ffn(h, oa, la, yb, p)
    return y.reshape(batch, seq, D_MODEL)


def kernel(x_prompt, x_sample, ffn1_norm, ffn1_w_gate, ffn1_w_up, ffn1_w_down, mix_norm, w_in, a_q_norm, a_k_norm, b_q_norm, b_k_norm, b_sink, w_proj_a, w_proj_b, w_gate, b_gate, w_out, ffn2_norm, ffn2_w_gate, ffn2_w_up, ffn2_w_down, final_norm):
    p = _prepare_params(*[w[0] for w in (
        ffn1_norm, ffn1_w_gate, ffn1_w_up, ffn1_w_down, mix_norm, w_in, a_q_norm, a_k_norm, b_q_norm,
        b_k_norm, b_sink, w_proj_a, w_proj_b, w_gate, b_gate, w_out, ffn2_norm, ffn2_w_gate, ffn2_w_up,
        ffn2_w_down, final_norm)])
    return (_encoder(x_prompt, p), _encoder(x_sample, p))
```

```python
import functools

import numpy as np
import jax
import jax.numpy as jnp
from jax import lax
from jax.experimental import pallas as pl
from jax.experimental.pallas import tpu as pltpu

D_MODEL = 1024
D_FF = 2816
HEAD_DIM = 64
A_PAIRS = ((128, 1), (512, 4), (2048, 16))
N_GROUPS = len(A_PAIRS)
A_SLOTS = 4
A_HEADS = A_SLOTS * N_GROUPS
A_OUT = A_SLOTS * HEAD_DIM
A_W = A_HEADS * HEAD_DIM
B_Q_HEADS = 8
B_KV_HEADS = 2
B_GROUP = B_Q_HEADS // B_KV_HEADS
B_HALF_WINDOW = 128
B_OUT = B_Q_HEADS * HEAD_DIM
B_KV = B_KV_HEADS * HEAD_DIM
NORM_EPS = 1e-6
MASK_VALUE = -1e30
LOG2_E = 1.4426950408889634
LN_2 = 0.6931471805599453

N_NORM = 2 * A_W + B_OUT + B_KV
N_VAL = A_W + B_KV
NORM_B_START = 2 * A_W
HEAD_LANES = 128
GROUP_W = 3 * A_OUT
QKVB_W = B_OUT + 2 * B_KV

LANES = 128
FF_CHUNK = 256
N_FF_CHUNKS = D_FF // FF_CHUNK
TOKEN_TILE = 256
ATTN_BQ = 128
ATTN_SUB = 4
A_HALF = 64
V7X_VMEM_LIMIT_BYTES = 56 * 1024 * 1024

_F32 = jnp.float32
_BF16 = jnp.bfloat16


def _rms(x, g):
    ms = jnp.mean(x * x, axis=-1, keepdims=True)
    return x * lax.rsqrt(ms + NORM_EPS) * g


def _swiglu_into(acc_ref, xn, wg_ref, wu_ref, wd_ref):
    acc_ref[...] = jnp.zeros_like(acc_ref)

    def body(c, carry):
        g = jnp.dot(xn, wg_ref[c], preferred_element_type=_F32)
        u = jnp.dot(xn, wu_ref[c], preferred_element_type=_F32)
        a = (g * jax.nn.sigmoid(g) * u).astype(_BF16)
        acc_ref[...] += jnp.dot(a, wd_ref[c], preferred_element_type=_F32)
        return carry

    lax.fori_loop(0, N_FF_CHUNKS, body, 0)


def _ffn_proj_kernel(x_ref, g1_ref, wg_ref, wu_ref, wd_ref, gmix_ref, win_ref, e_ref, et_ref,
                     gain_ref, h_ref, a0_ref, a1_ref, a2_ref, qkvb_ref, acc_ref, slab_ref):
    tm = x_ref.shape[0]
    x = x_ref[...]
    xn = _rms(x, g1_ref[...]).astype(_BF16)
    _swiglu_into(acc_ref, xn, wg_ref, wu_ref, wd_ref)
    h = x + 0.5 * acc_ref[...]
    h_ref[...] = h
    u = _rms(h, gmix_ref[...]).astype(_BF16)
    qkv = jnp.dot(u, win_ref[...], preferred_element_type=_F32)
    nrm = qkv[:, :N_NORM]
    ssq = jnp.dot((nrm * nrm).astype(_BF16), e_ref[...], preferred_element_type=_F32)
    inv = lax.rsqrt(ssq * (1.0 / HEAD_DIM) + NORM_EPS)
    inv_hi = inv.astype(_BF16)
    inv_lo = (inv - inv_hi.astype(_F32)).astype(_BF16)
    inv_b = jnp.dot(jnp.concatenate([inv_hi, inv_lo], axis=1), et_ref[...],
                    preferred_element_type=_F32)
    nq = nrm * inv_b * gain_ref[...]
    val = qkv[:, N_NORM:]
    a0_ref[0, :, :2 * A_OUT] = nq[:, :2 * A_OUT].astype(_BF16)
    a0_ref[0, :, 2 * A_OUT:] = val[:, :A_OUT].astype(_BF16)
    n_slab = GROUP_W // LANES
    for g, a_ref in ((1, a1_ref), (2, a2_ref)):
        d = A_PAIRS[g][1]
        base = (g - 1) * n_slab
        for s in range(n_slab):
            lo = s * LANES
            if lo < 2 * A_OUT:
                src = nq[:, 2 * A_OUT * g + lo:2 * A_OUT * g + lo + LANES]
            else:
                src = val[:, A_OUT * g + lo - 2 * A_OUT:A_OUT * g + lo - 2 * A_OUT + LANES]
            slab_ref[base + s] = src
        for r in range(d):
            for s in range(n_slab):
                a_ref[r, :, s * LANES:(s + 1) * LANES] = (
                    slab_ref[base + s, pl.ds(r, tm // d, stride=d), :].astype(_BF16))
    qkvb_ref[:, :B_OUT + B_KV] = nq[:, NORM_B_START:].astype(_BF16)
    qkvb_ref[:, B_OUT + B_KV:] = val[:, A_W:].astype(_BF16)


def _band_bias(rows_per_head, n_heads, bk, half, slopes, dist_scale, key_lo, key_hi):
    shape = (n_heads * rows_per_head, bk)
    row = lax.broadcasted_iota(jnp.int32, shape, 0)
    col = lax.broadcasted_iota(jnp.int32, shape, 1)
    rel = jnp.abs(col - half - (row % rows_per_head))
    head = row // rows_per_head
    slope = jnp.full(shape, slopes[0], _F32)
    for h in range(1, n_heads):
        slope = jnp.where(head == h, slopes[h], slope)
    dist = (rel * dist_scale).astype(_F32)
    ok = (rel <= half) & (col >= key_lo) & (col < key_hi)
    return jnp.where(ok, -slope * LOG2_E * dist, MASK_VALUE)


def _init_bias_tables(bias_ref, rows_per_head, n_heads, half, slopes, dist_scale):
    bk = rows_per_head + 2 * half
    for t, (lo, hi) in enumerate(((0, bk), (half, bk), (0, rows_per_head + half))):
        bias_ref[t] = _band_bias(rows_per_head, n_heads, bk, half, slopes, dist_scale, lo, hi)


def _table_index(j, step, n_steps):
    if j == 0:
        return jnp.where(step == 0, 1, 0)
    if j == ATTN_SUB - 1:
        return jnp.where(step == n_steps - 1, 2, 0)
    return 0


def _attn_a_kernel(q_ref, kp_ref, ko_ref, kn_ref, vp_ref, vo_ref, vn_ref, o_ref, lse_ref, bias_ref,
                   *, dilation, slopes, n_steps):
    step = pl.program_id(2)
    bq = ATTN_BQ
    bk = bq + 2 * A_HALF

    @pl.when((pl.program_id(0) == 0) & (pl.program_id(1) == 0) & (step == 0))
    def _():
        _init_bias_tables(bias_ref, bq, A_SLOTS, A_HALF, slopes, dilation)

    k_all = jnp.concatenate([kp_ref[...], ko_ref[...], kn_ref[...]], axis=0)
    v_all = jnp.concatenate([vp_ref[...], vo_ref[...], vn_ref[...]], axis=0)
    lane_head = lax.broadcasted_iota(jnp.int32, (bq, A_OUT), 1) // HEAD_DIM
    for j in range(ATTN_SUB):
        q = q_ref[j * bq:(j + 1) * bq, :]
        zero = jnp.zeros_like(q)
        qs = jnp.concatenate([jnp.where(lane_head == h, q, zero) for h in range(A_SLOTS)], axis=0)
        k = k_all[j * bq:j * bq + bk]
        v = v_all[j * bq:j * bq + bk]
        s = lax.dot_general(qs, k, (((1,), (1,)), ((), ())), preferred_element_type=_F32)
        s = s + bias_ref[_table_index(j, step, n_steps)]
        m = jnp.max(s, axis=1, keepdims=True)
        p = jnp.exp2(s - m)
        l = jnp.sum(p, axis=1, keepdims=True)
        pv = jnp.dot(p.astype(_BF16), v, preferred_element_type=_F32)
        on = pv * (1.0 / l)
        lse = (m + jnp.log2(l)) * LN_2
        o = on[:bq]
        lse_t = jnp.broadcast_to(lse[:bq], (bq, A_OUT))
        for h in range(1, A_SLOTS):
            sel = lane_head == h
            o = jnp.where(sel, on[h * bq:(h + 1) * bq], o)
            lse_t = jnp.where(sel, jnp.broadcast_to(lse[h * bq:(h + 1) * bq], (bq, A_OUT)), lse_t)
        o_ref[j * bq:(j + 1) * bq, :] = o.astype(o_ref.dtype)
        lse_ref[j * bq:(j + 1) * bq, :] = lse_t


def _attn_b_kernel(q_ref, kp_ref, ko_ref, kn_ref, vp_ref, vo_ref, vn_ref, sink_ref, o_ref, bias_ref,
                   *, slopes, n_steps):
    step = pl.program_id(1)
    bq = ATTN_BQ
    bk = bq + 2 * B_HALF_WINDOW

    @pl.when((pl.program_id(0) == 0) & (step == 0))
    def _():
        _init_bias_tables(bias_ref, bq, B_Q_HEADS, B_HALF_WINDOW, slopes, 1)

    k_all = jnp.concatenate([kp_ref[...], ko_ref[...], kn_ref[...]], axis=0)
    v_all = jnp.concatenate([vp_ref[...], vo_ref[...], vn_ref[...]], axis=0)
    v_all = jnp.concatenate([v_all, jnp.ones_like(v_all)], axis=1)
    lane_kv = lax.broadcasted_iota(jnp.int32, (bq, B_KV), 1) // HEAD_DIM
    for j in range(ATTN_SUB):
        parts = []
        for kv in range(B_KV_HEADS):
            for c in range(B_GROUP):
                chunk = q_ref[j * bq:(j + 1) * bq, c * B_KV:(c + 1) * B_KV]
                parts.append(jnp.where(lane_kv == kv, chunk, jnp.zeros_like(chunk)))
        qs = jnp.concatenate(parts, axis=0)
        k = k_all[j * bq:j * bq + bk]
        v = v_all[j * bq:j * bq + bk]
        s = lax.dot_general(qs, k, (((1,), (1,)), ((), ())), preferred_element_type=_F32)
        s = s + bias_ref[_table_index(j, step, n_steps)]
        ms, ps = [], []
        for h in range(B_Q_HEADS):
            s_h = s[h * bq:(h + 1) * bq]
            m_h = jnp.maximum(jnp.max(s_h, axis=1, keepdims=True), sink_ref[h])
            ps.append(jnp.exp2(s_h - m_h).astype(_BF16))
            ms.append(m_h)
        pv = jnp.dot(jnp.concatenate(ps, axis=0), v, preferred_element_type=_F32)

        def head_out(h):
            rows = pv[h * bq:(h + 1) * bq]
            return rows[:, :B_KV] / (rows[:, B_KV:] + jnp.exp2(sink_ref[h] - ms[h]))

        for c in range(B_GROUP):
            o_ref[j * bq:(j + 1) * bq, c * B_KV:(c + 1) * B_KV] = jnp.where(
                lane_kv == 0, head_out(c), head_out(B_GROUP + c)).astype(o_ref.dtype)


def _natural_order(ref, dilation, slab_ref, base):
    n_slab = A_OUT // LANES
    if dilation == 1:
        x = ref[0].astype(_F32)
        return [x[:, s * LANES:(s + 1) * LANES] for s in range(n_slab)]
    rows = ref.shape[1]
    for r in range(dilation):
        blk = ref[r].astype(_F32)
        for s in range(n_slab):
            slab_ref[base + s, pl.ds(r, rows, stride=dilation), :] = blk[:, s * LANES:(s + 1) * LANES]
    return [slab_ref[base + s] for s in range(n_slab)]


def _mix_ffn_kernel(h_ref, oa0_ref, la0_ref, oa1_ref, la1_ref, oa2_ref, la2_ref, yb_ref,
                    gmix_ref, wgate_ref, bgate_ref, wpa_ref, wpb_ref, wout_ref,
                    g2_ref, wg_ref, wu_ref, wd_ref, gfin_ref, y_ref, acc_ref, slab_ref):
    h = h_ref[...]
    u = _rms(h, gmix_ref[...]).astype(_BF16)
    gates = jax.nn.sigmoid(jnp.dot(u, wgate_ref[...], preferred_element_type=_F32) + bgate_ref[...])
    n_slab = A_OUT // LANES
    outs, lses = [], []
    for g, (o_ref, l_ref) in enumerate(((oa0_ref, la0_ref), (oa1_ref, la1_ref), (oa2_ref, la2_ref))):
        d = A_PAIRS[g][1]
        outs.append(_natural_order(o_ref, d, slab_ref, 2 * g * n_slab))
        lses.append(_natural_order(l_ref, d, slab_ref, (2 * g + 1) * n_slab))
    ya = []
    for s in range(n_slab):
        la0, la1, la2 = lses[0][s], lses[1][s], lses[2][s]
        mx = jnp.maximum(jnp.maximum(la0, la1), la2)
        e0, e1, e2 = jnp.exp(la0 - mx), jnp.exp(la1 - mx), jnp.exp(la2 - mx)
        ya.append((e0 * outs[0][s] + e1 * outs[1][s] + e2 * outs[2][s]) / (e0 + e1 + e2))
    ya = jnp.concatenate(ya, axis=1)
    pa = jnp.dot(ya.astype(_BF16), wpa_ref[...], preferred_element_type=_F32)
    pb = jnp.dot(yb_ref[...], wpb_ref[...], preferred_element_type=_F32)
    mixed = gates[:, :D_MODEL] * pa + gates[:, D_MODEL:] * pb
    h = h + jnp.dot(mixed.astype(_BF16), wout_ref[...], preferred_element_type=_F32)
    hn = _rms(h, g2_ref[...]).astype(_BF16)
    _swiglu_into(acc_ref, hn, wg_ref, wu_ref, wd_ref)
    h = h + 0.5 * acc_ref[...]
    y_ref[...] = _rms(h, gfin_ref[...])


def _resident(shape):
    nd = len(shape)
    return pl.BlockSpec(shape, lambda *_: (0,) * nd, pipeline_mode=pl.Buffered(1))


def _residue_spec(tm, d, width):
    return pl.BlockSpec((None, d, tm // d, width), lambda b, i: (b, 0, i, 0))


def _ffn_proj(x, p):
    batch, seq, _ = x.shape
    tm = TOKEN_TILE
    row = lambda w: pl.BlockSpec((None, tm, w), lambda b, i: (b, i, 0))
    consts = (p["g1"], p["wg1"], p["wu1"], p["wd1"], p["gmix"], p["win"], p["e"], p["et"], p["gain"])
    dils = [d for _, d in A_PAIRS]
    return pl.pallas_call(
        _ffn_proj_kernel,
        grid=(batch, seq // tm),
        in_specs=[row(D_MODEL)] + [_resident(c.shape) for c in consts],
        out_specs=[row(D_MODEL)] + [_residue_spec(tm, d, GROUP_W) for d in dils] + [row(QKVB_W)],
        out_shape=[jax.ShapeDtypeStruct((batch, seq, D_MODEL), _F32)]
        + [jax.ShapeDtypeStruct((batch, d, seq // d, GROUP_W), _BF16) for d in dils]
        + [jax.ShapeDtypeStruct((batch, seq, QKVB_W), _BF16)],
        scratch_shapes=[pltpu.VMEM((tm, D_MODEL), _F32),
                        pltpu.VMEM((2 * GROUP_W // LANES, tm, LANES), _F32)],
        compiler_params=pltpu.CompilerParams(
            dimension_semantics=("arbitrary",) * 2, vmem_limit_bytes=V7X_VMEM_LIMIT_BYTES),
        name="ffn_proj",
    )(x, *consts)


def _attn_a(qkv, group):
    batch, d, length, _ = qkv.shape
    rows = ATTN_BQ * ATTN_SUB
    n_steps = length // rows
    halo = rows // A_HALF
    slopes = [float(2.0 ** (-8.0 * (group * A_SLOTS + h + 1) / A_HEADS)) for h in range(A_SLOTS)]

    def own(col):
        return pl.BlockSpec((None, None, rows, A_OUT), lambda b, r, n: (b, r, n, col))

    def prev(col):
        return pl.BlockSpec((None, None, A_HALF, A_OUT),
                            lambda b, r, n: (b, r, jnp.maximum(n * halo - 1, 0), col))

    def nxt(col):
        return pl.BlockSpec((None, None, A_HALF, A_OUT),
                            lambda b, r, n: (b, r, jnp.minimum((n + 1) * halo, n_steps * halo - 1), col))

    return pl.pallas_call(
        functools.partial(_attn_a_kernel, dilation=d, slopes=slopes, n_steps=n_steps),
        grid=(batch, d, n_steps),
        in_specs=[own(0), prev(1), own(1), nxt(1), prev(2), own(2), nxt(2)],
        out_specs=[own(0), own(0)],
        out_shape=[jax.ShapeDtypeStruct((batch, d, length, A_OUT), _BF16),
                   jax.ShapeDtypeStruct((batch, d, length, A_OUT), _F32)],
        scratch_shapes=[pltpu.VMEM((3, A_SLOTS * ATTN_BQ, ATTN_BQ + 2 * A_HALF), _F32)],
        compiler_params=pltpu.CompilerParams(dimension_semantics=("arbitrary",) * 3),
        name=f"attn_a{group}",
    )(qkv, qkv, qkv, qkv, qkv, qkv, qkv)


def _attn_b(qkvb, sink):
    batch, seq, _ = qkvb.shape
    rows = ATTN_BQ * ATTN_SUB
    n_steps = seq // rows
    halo = rows // B_HALF_WINDOW
    slopes = [float(2.0 ** (-8.0 * (h + 1) / B_Q_HEADS)) for h in range(B_Q_HEADS)]
    k_col = B_OUT // B_KV

    def own(col):
        return pl.BlockSpec((None, rows, B_KV), lambda b, n: (b, n, col))

    def prev(col):
        return pl.BlockSpec((None, B_HALF_WINDOW, B_KV),
                            lambda b, n: (b, jnp.maximum(n * halo - 1, 0), col))

    def nxt(col):
        return pl.BlockSpec((None, B_HALF_WINDOW, B_KV),
                            lambda b, n: (b, jnp.minimum((n + 1) * halo, n_steps * halo - 1), col))

    q_spec = pl.BlockSpec((None, rows, B_OUT), lambda b, n: (b, n, 0))
    return pl.pallas_call(
        functools.partial(_attn_b_kernel, slopes=slopes, n_steps=n_steps),
        grid=(batch, n_steps),
        in_specs=[q_spec, prev(k_col), own(k_col), nxt(k_col),
                  prev(k_col + 1), own(k_col + 1), nxt(k_col + 1),
                  pl.BlockSpec(memory_space=pltpu.SMEM)],
        out_specs=q_spec,
        out_shape=jax.ShapeDtypeStruct((batch, seq, B_OUT), _BF16),
        scratch_shapes=[pltpu.VMEM((3, B_Q_HEADS * ATTN_BQ, ATTN_BQ + 2 * B_HALF_WINDOW), _F32)],
        compiler_params=pltpu.CompilerParams(dimension_semantics=("arbitrary",) * 2),
        name="attn_b",
    )(qkvb, qkvb, qkvb, qkvb, qkvb, qkvb, qkvb, sink)


def _mix_ffn(h, oa, la, yb, p):
    batch, seq, _ = h.shape
    tm = TOKEN_TILE
    row = lambda w: pl.BlockSpec((None, tm, w), lambda b, i: (b, i, 0))
    consts = (p["gmix"], p["wgate"], p["bgate"], p["wpa"], p["wpb"], p["wout"],
              p["g2"], p["wg2"], p["wu2"], p["wd2"], p["gfin"])
    attn_specs, attn_args = [], []
    for g, (_, d) in enumerate(A_PAIRS):
        attn_specs += [_residue_spec(tm, d, A_OUT)] * 2
        attn_args += [oa[g], la[g]]
    return pl.pallas_call(
        _mix_ffn_kernel,
        grid=(batch, seq // tm),
        in_specs=[row(D_MODEL)] + attn_specs + [row(B_OUT)] + [_resident(c.shape) for c in consts],
        out_specs=row(D_MODEL),
        out_shape=jax.ShapeDtypeStruct((batch, seq, D_MODEL), _F32),
        scratch_shapes=[pltpu.VMEM((tm, D_MODEL), _F32),
                        pltpu.VMEM((2 * N_GROUPS * A_OUT // LANES, tm, LANES), _F32)],
        compiler_params=pltpu.CompilerParams(
            dimension_semantics=("arbitrary",) * 2, vmem_limit_bytes=V7X_VMEM_LIMIT_BYTES),
        name="mix_ffn",
    )(h, *attn_args, yb, *consts)


def _b_head_perm():
    idx = [(kv * B_GROUP + c) * HEAD_DIM + e
           for c in range(B_GROUP) for kv in range(B_KV_HEADS) for e in range(HEAD_DIM)]
    return np.asarray(idx, dtype=np.int32)


def _prepare_params(ffn1_norm, ffn1_w_gate, ffn1_w_up, ffn1_w_down, mix_norm, w_in, a_q_norm,
                    a_k_norm, b_q_norm, b_k_norm, b_sink, w_proj_a, w_proj_b, w_gate, b_gate, w_out,
                    ffn2_norm, ffn2_w_gate, ffn2_w_up, ffn2_w_down, final_norm):
    def vec(g):
        return g.reshape(1, -1).astype(_F32)

    def chunk_cols(w):
        return w.astype(_BF16).reshape(D_MODEL, N_FF_CHUNKS, FF_CHUNK).transpose(1, 0, 2)

    def chunk_rows(w):
        return w.astype(_BF16).reshape(N_FF_CHUNKS, FF_CHUNK, D_MODEL)

    perm = _b_head_perm()
    qa, ka, va, qb, kb, vb = jnp.split(w_in, np.cumsum([A_W, A_W, A_W, B_OUT, B_KV]).tolist(), axis=1)
    grp = lambda w, g: w[:, g * A_OUT:(g + 1) * A_OUT]
    norm_cols = [m for g in range(N_GROUPS) for m in (grp(qa, g), grp(ka, g))] + [qb[:, perm], kb]
    win = jnp.concatenate(norm_cols + [va, vb], axis=1).astype(_BF16)
    scale = HEAD_DIM ** -0.5 * LOG2_E
    gain = jnp.concatenate(
        [jnp.tile(g, A_SLOTS) for _ in range(N_GROUPS) for g in (a_q_norm * scale, a_k_norm)]
        + [jnp.tile(b_q_norm * scale, B_Q_HEADS), jnp.tile(b_k_norm, B_KV_HEADS)])
    head_of_col = np.arange(N_NORM) // HEAD_DIM
    member = (head_of_col[:, None] == np.arange(HEAD_LANES)[None, :])
    e = jnp.asarray(member, dtype=_BF16)
    et = jnp.asarray(np.concatenate([member.T, member.T], axis=0), dtype=_BF16)
    return dict(
        g1=vec(ffn1_norm), wg1=chunk_cols(ffn1_w_gate), wu1=chunk_cols(ffn1_w_up), wd1=chunk_rows(ffn1_w_down),
        gmix=vec(mix_norm), win=win, e=e, et=et, gain=vec(gain), sink=b_sink.astype(_F32) * LOG2_E,
        wgate=w_gate.astype(_BF16), bgate=vec(b_gate), wpa=w_proj_a.astype(_BF16),
        wpb=w_proj_b[perm, :].astype(_BF16), wout=w_out.astype(_BF16),
        g2=vec(ffn2_norm), wg2=chunk_cols(ffn2_w_gate), wu2=chunk_cols(ffn2_w_up), wd2=chunk_rows(ffn2_w_down),
        gfin=vec(final_norm),
    )


def _encoder(x, p):
    h, a0, a1, a2, qkvb = _ffn_proj(x, p)
    oa, la = zip(*[_attn_a(a, g) for g, a in enumerate((a0, a1, a2))])
    yb = _attn_b(qkvb, p["sink"])
    return _mix_ffn(h, oa, la, yb, p)


def kernel(x_prompt, x_sample, ffn1_norm, ffn1_w_gate, ffn1_w_up, ffn1_w_down, mix_norm, w_in, a_q_norm, a_k_norm, b_q_norm, b_k_norm, b_sink, w_proj_a, w_proj_b, w_gate, b_gate, w_out, ffn2_norm, ffn2_w_gate, ffn2_w_up, ffn2_w_down, final_norm):
    p = _prepare_params(*[w[0] for w in (
        ffn1_norm, ffn1_w_gate, ffn1_w_up, ffn1_w_down, mix_norm, w_in, a_q_norm, a_k_norm, b_q_norm,
        b_k_norm, b_sink, w_proj_a, w_proj_b, w_gate, b_gate, w_out, ffn2_norm, ffn2_w_gate, ffn2_w_up,
        ffn2_w_down, final_norm)])
    return (_encoder(x_prompt, p), _encoder(x_sample, p))
```

```python
import functools

import numpy as np
import jax
import jax.numpy as jnp
from jax import lax
from jax.experimental import pallas as pl
from jax.experimental.pallas import tpu as pltpu

D_MODEL = 1024
D_FF = 2816
HEAD_DIM = 64
A_PAIRS = ((128, 1), (512, 4), (2048, 16))
N_GROUPS = len(A_PAIRS)
A_SLOTS = 4
A_HEADS = A_SLOTS * N_GROUPS
A_OUT = A_SLOTS * HEAD_DIM
A_W = A_HEADS * HEAD_DIM
B_Q_HEADS = 8
B_KV_HEADS = 2
B_GROUP = B_Q_HEADS // B_KV_HEADS
B_HALF_WINDOW = 128
B_OUT = B_Q_HEADS * HEAD_DIM
B_KV = B_KV_HEADS * HEAD_DIM
NORM_EPS = 1e-6
MASK_VALUE = -1e30
LOG2_E = 1.4426950408889634
LN_2 = 0.6931471805599453

N_NORM = 2 * A_W + B_OUT + B_KV
N_VAL = A_W + B_KV
NORM_B_START = 2 * A_W
HEAD_LANES = 128
GROUP_W = 3 * A_OUT
QKVB_W = B_OUT + 2 * B_KV

LANES = 128
FF_CHUNK = 256
N_FF_CHUNKS = D_FF // FF_CHUNK
TOKEN_TILE = 512
ATTN_BQ = 128
ATTN_SUB = 4
A_HALF = 64
V7X_VMEM_LIMIT_BYTES = 56 * 1024 * 1024

_F32 = jnp.float32
_BF16 = jnp.bfloat16


def _rms(x, g):
    ms = jnp.mean(x * x, axis=-1, keepdims=True)
    return x * lax.rsqrt(ms + NORM_EPS) * g


def _swiglu_into(acc_ref, xn, wg_ref, wu_ref, wd_ref):
    acc_ref[...] = jnp.zeros_like(acc_ref)

    def body(c, carry):
        g = jnp.dot(xn, wg_ref[c], preferred_element_type=_F32)
        u = jnp.dot(xn, wu_ref[c], preferred_element_type=_F32)
        a = (g * jax.nn.sigmoid(g) * u).astype(_BF16)
        acc_ref[...] += jnp.dot(a, wd_ref[c], preferred_element_type=_F32)
        return carry

    lax.fori_loop(0, N_FF_CHUNKS, body, 0, unroll=True)


def _ffn_proj_kernel(x_ref, g1_ref, wg_ref, wu_ref, wd_ref, gmix_ref, win_ref, e_ref, et_ref,
                     gain_ref, h_ref, a0_ref, a1_ref, a2_ref, qkvb_ref, acc_ref, slab_ref):
    tm = x_ref.shape[0]
    x = x_ref[...]
    xn = _rms(x, g1_ref[...]).astype(_BF16)
    _swiglu_into(acc_ref, xn, wg_ref, wu_ref, wd_ref)
    h = x + 0.5 * acc_ref[...]
    h_ref[...] = h
    u = _rms(h, gmix_ref[...]).astype(_BF16)
    qkv = jnp.dot(u, win_ref[...], preferred_element_type=_F32)
    nrm = qkv[:, :N_NORM]
    ssq = jnp.dot((nrm * nrm).astype(_BF16), e_ref[...], preferred_element_type=_F32)
    inv = lax.rsqrt(ssq * (1.0 / HEAD_DIM) + NORM_EPS)
    inv_hi = inv.astype(_BF16)
    inv_lo = (inv - inv_hi.astype(_F32)).astype(_BF16)
    inv_b = jnp.dot(jnp.concatenate([inv_hi, inv_lo], axis=1), et_ref[...],
                    preferred_element_type=_F32)
    nq = nrm * inv_b * gain_ref[...]
    val = qkv[:, N_NORM:]
    a0_ref[0, :, :2 * A_OUT] = nq[:, :2 * A_OUT].astype(_BF16)
    a0_ref[0, :, 2 * A_OUT:] = val[:, :A_OUT].astype(_BF16)
    n_slab = GROUP_W // LANES
    for g, a_ref in ((1, a1_ref), (2, a2_ref)):
        d = A_PAIRS[g][1]
        base = (g - 1) * n_slab
        for s in range(n_slab):
            lo = s * LANES
            if lo < 2 * A_OUT:
                src = nq[:, 2 * A_OUT * g + lo:2 * A_OUT * g + lo + LANES]
            else:
                src = val[:, A_OUT * g + lo - 2 * A_OUT:A_OUT * g + lo - 2 * A_OUT + LANES]
            slab_ref[base + s] = src
        for r in range(d):
            for s in range(n_slab):
                a_ref[r, :, s * LANES:(s + 1) * LANES] = (
                    slab_ref[base + s, pl.ds(r, tm // d, stride=d), :].astype(_BF16))
    qkvb_ref[:, :B_OUT + B_KV] = nq[:, NORM_B_START:].astype(_BF16)
    qkvb_ref[:, B_OUT + B_KV:] = val[:, A_W:].astype(_BF16)


def _band_bias(rows_per_head, n_heads, bk, half, slopes, dist_scale, key_lo, key_hi):
    shape = (n_heads * rows_per_head, bk)
    row = lax.broadcasted_iota(jnp.int32, shape, 0)
    col = lax.broadcasted_iota(jnp.int32, shape, 1)
    rel = jnp.abs(col - half - (row % rows_per_head))
    head = row // rows_per_head
    slope = jnp.full(shape, slopes[0], _F32)
    for h in range(1, n_heads):
        slope = jnp.where(head == h, slopes[h], slope)
    dist = (rel * dist_scale).astype(_F32)
    ok = (rel <= half) & (col >= key_lo) & (col < key_hi)
    return jnp.where(ok, -slope * LOG2_E * dist, MASK_VALUE)


def _init_bias_tables(bias_ref, rows_per_head, n_heads, half, slopes, dist_scale):
    bk = rows_per_head + 2 * half
    for t, (lo, hi) in enumerate(((0, bk), (half, bk), (0, rows_per_head + half))):
        bias_ref[t] = _band_bias(rows_per_head, n_heads, bk, half, slopes, dist_scale, lo, hi)


def _table_index(j, step, n_steps):
    if j == 0:
        return jnp.where(step == 0, 1, 0)
    if j == ATTN_SUB - 1:
        return jnp.where(step == n_steps - 1, 2, 0)
    return 0


def _attn_a_kernel(q_ref, kp_ref, ko_ref, kn_ref, vp_ref, vo_ref, vn_ref, o_ref, lse_ref, bias_ref,
                   *, dilation, slopes, n_steps):
    step = pl.program_id(2)
    bq = ATTN_BQ
    bk = bq + 2 * A_HALF

    @pl.when((pl.program_id(0) == 0) & (pl.program_id(1) == 0) & (step == 0))
    def _():
        _init_bias_tables(bias_ref, bq, A_SLOTS, A_HALF, slopes, dilation)

    k_all = jnp.concatenate([kp_ref[...], ko_ref[...], kn_ref[...]], axis=0)
    v_all = jnp.concatenate([vp_ref[...], vo_ref[...], vn_ref[...]], axis=0)
    lane_head = lax.broadcasted_iota(jnp.int32, (bq, A_OUT), 1) // HEAD_DIM
    for j in range(ATTN_SUB):
        q = q_ref[j * bq:(j + 1) * bq, :]
        zero = jnp.zeros_like(q)
        qs = jnp.concatenate([jnp.where(lane_head == h, q, zero) for h in range(A_SLOTS)], axis=0)
        k = k_all[j * bq:j * bq + bk]
        v = v_all[j * bq:j * bq + bk]
        s = lax.dot_general(qs, k, (((1,), (1,)), ((), ())), preferred_element_type=_F32)
        s = s + bias_ref[_table_index(j, step, n_steps)]
        m = jnp.max(s, axis=1, keepdims=True)
        p = jnp.exp2(s - m)
        l = jnp.sum(p, axis=1, keepdims=True)
        pv = jnp.dot(p.astype(_BF16), v, preferred_element_type=_F32)
        on = pv * (1.0 / l)
        lse = (m + jnp.log2(l)) * LN_2
        o = on[:bq]
        lse_t = jnp.broadcast_to(lse[:bq], (bq, A_OUT))
        for h in range(1, A_SLOTS):
            sel = lane_head == h
            o = jnp.where(sel, on[h * bq:(h + 1) * bq], o)
            lse_t = jnp.where(sel, jnp.broadcast_to(lse[h * bq:(h + 1) * bq], (bq, A_OUT)), lse_t)
        o_ref[j * bq:(j + 1) * bq, :] = o.astype(o_ref.dtype)
        lse_ref[j * bq:(j + 1) * bq, :] = lse_t


def _attn_b_kernel(q_ref, kp_ref, ko_ref, kn_ref, vp_ref, vo_ref, vn_ref, sink_ref, o_ref, bias_ref,
                   *, slopes, n_steps):
    step = pl.program_id(1)
    bq = ATTN_BQ
    bk = bq + 2 * B_HALF_WINDOW

    @pl.when((pl.program_id(0) == 0) & (step == 0))
    def _():
        _init_bias_tables(bias_ref, bq, B_Q_HEADS, B_HALF_WINDOW, slopes, 1)

    k_all = jnp.concatenate([kp_ref[...], ko_ref[...], kn_ref[...]], axis=0)
    v_all = jnp.concatenate([vp_ref[...], vo_ref[...], vn_ref[...]], axis=0)
    v_all = jnp.concatenate([v_all, jnp.ones_like(v_all)], axis=1)
    lane_kv = lax.broadcasted_iota(jnp.int32, (bq, B_KV), 1) // HEAD_DIM
    for j in range(ATTN_SUB):
        parts = []
        for kv in range(B_KV_HEADS):
            for c in range(B_GROUP):
                chunk = q_ref[j * bq:(j + 1) * bq, c * B_KV:(c + 1) * B_KV]
                parts.append(jnp.where(lane_kv == kv, chunk, jnp.zeros_like(chunk)))
        qs = jnp.concatenate(parts, axis=0)
        k = k_all[j * bq:j * bq + bk]
        v = v_all[j * bq:j * bq + bk]
        s = lax.dot_general(qs, k, (((1,), (1,)), ((), ())), preferred_element_type=_F32)
        s = s + bias_ref[_table_index(j, step, n_steps)]
        ms, ps = [], []
        for h in range(B_Q_HEADS):
            s_h = s[h * bq:(h + 1) * bq]
            m_h = jnp.maximum(jnp.max(s_h, axis=1, keepdims=True), sink_ref[h])
            ps.append(jnp.exp2(s_h - m_h).astype(_BF16))
            ms.append(m_h)
        pv = jnp.dot(jnp.concatenate(ps, axis=0), v, preferred_element_type=_F32)

        def head_out(h):
            rows = pv[h * bq:(h + 1) * bq]
            return rows[:, :B_KV] / (rows[:, B_KV:] + jnp.exp2(sink_ref[h] - ms[h]))

        for c in range(B_GROUP):
            o_ref[j * bq:(j + 1) * bq, c * B_KV:(c + 1) * B_KV] = jnp.where(
                lane_kv == 0, head_out(c), head_out(B_GROUP + c)).astype(o_ref.dtype)


def _natural_order(ref, dilation, slab_ref, base):
    n_slab = A_OUT // LANES
    if dilation == 1:
        x = ref[0].astype(_F32)
        return [x[:, s * LANES:(s + 1) * LANES] for s in range(n_slab)]
    rows = ref.shape[1]
    for r in range(dilation):
        blk = ref[r].astype(_F32)
        for s in range(n_slab):
            slab_ref[base + s, pl.ds(r, rows, stride=dilation), :] = blk[:, s * LANES:(s + 1) * LANES]
    return [slab_ref[base + s] for s in range(n_slab)]


def _mix_ffn_kernel(h_ref, oa0_ref, la0_ref, oa1_ref, la1_ref, oa2_ref, la2_ref, yb_ref,
                    gmix_ref, wgate_ref, bgate_ref, wpa_ref, wpb_ref, wout_ref,
                    g2_ref, wg_ref, wu_ref, wd_ref, gfin_ref, y_ref, acc_ref, slab_ref):
    h = h_ref[...]
    u = _rms(h, gmix_ref[...]).astype(_BF16)
    gates = jax.nn.sigmoid(jnp.dot(u, wgate_ref[...], preferred_element_type=_F32) + bgate_ref[...])
    n_slab = A_OUT // LANES
    outs, lses = [], []
    for g, (o_ref, l_ref) in enumerate(((oa0_ref, la0_ref), (oa1_ref, la1_ref), (oa2_ref, la2_ref))):
        d = A_PAIRS[g][1]
        outs.append(_natural_order(o_ref, d, slab_ref, 2 * g * n_slab))
        lses.append(_natural_order(l_ref, d, slab_ref, (2 * g + 1) * n_slab))
    ya = []
    for s in range(n_slab):
        la0, la1, la2 = lses[0][s], lses[1][s], lses[2][s]
        mx = jnp.maximum(jnp.maximum(la0, la1), la2)
        e0, e1, e2 = jnp.exp(la0 - mx), jnp.exp(la1 - mx), jnp.exp(la2 - mx)
        ya.append((e0 * outs[0][s] + e1 * outs[1][s] + e2 * outs[2][s]) / (e0 + e1 + e2))
    ya = jnp.concatenate(ya, axis=1)
    pa = jnp.dot(ya.astype(_BF16), wpa_ref[...], preferred_element_type=_F32)
    pb = jnp.dot(yb_ref[...], wpb_ref[...], preferred_element_type=_F32)
    mixed = gates[:, :D_MODEL] * pa + gates[:, D_MODEL:] * pb
    h = h + jnp.dot(mixed.astype(_BF16), wout_ref[...], preferred_element_type=_F32)
    hn = _rms(h, g2_ref[...]).astype(_BF16)
    _swiglu_into(acc_ref, hn, wg_ref, wu_ref, wd_ref)
    h = h + 0.5 * acc_ref[...]
    y_ref[...] = _rms(h, gfin_ref[...])


def _resident(shape):
    nd = len(shape)
    return pl.BlockSpec(shape, lambda *_: (0,) * nd, pipeline_mode=pl.Buffered(1))


def _residue_spec(tm, d, width):
    return pl.BlockSpec((None, d, tm // d, width), lambda b, i: (b, 0, i, 0))


def _ffn_proj(x, p):
    batch, seq, _ = x.shape
    tm = TOKEN_TILE
    row = lambda w: pl.BlockSpec((None, tm, w), lambda b, i: (b, i, 0))
    consts = (p["g1"], p["wg1"], p["wu1"], p["wd1"], p["gmix"], p["win"], p["e"], p["et"], p["gain"])
    dils = [d for _, d in A_PAIRS]
    return pl.pallas_call(
        _ffn_proj_kernel,
        grid=(batch, seq // tm),
        in_specs=[row(D_MODEL)] + [_resident(c.shape) for c in consts],
        out_specs=[row(D_MODEL)] + [_residue_spec(tm, d, GROUP_W) for d in dils] + [row(QKVB_W)],
        out_shape=[jax.ShapeDtypeStruct((batch, seq, D_MODEL), _F32)]
        + [jax.ShapeDtypeStruct((batch, d, seq // d, GROUP_W), _BF16) for d in dils]
        + [jax.ShapeDtypeStruct((batch, seq, QKVB_W), _BF16)],
        scratch_shapes=[pltpu.VMEM((tm, D_MODEL), _F32),
                        pltpu.VMEM((2 * GROUP_W // LANES, tm, LANES), _F32)],
        compiler_params=pltpu.CompilerParams(
            dimension_semantics=("arbitrary",) * 2, vmem_limit_bytes=V7X_VMEM_LIMIT_BYTES),
        name="ffn_proj",
    )(x, *consts)


def _attn_a(qkv, group):
    batch, d, length, _ = qkv.shape
    rows = ATTN_BQ * ATTN_SUB
    n_steps = length // rows
    halo = rows // A_HALF
    slopes = [float(2.0 ** (-8.0 * (group * A_SLOTS + h + 1) / A_HEADS)) for h in range(A_SLOTS)]

    def own(col):
        return pl.BlockSpec((None, None, rows, A_OUT), lambda b, r, n: (b, r, n, col))

    def prev(col):
        return pl.BlockSpec((None, None, A_HALF, A_OUT),
                            lambda b, r, n: (b, r, jnp.maximum(n * halo - 1, 0), col))

    def nxt(col):
        return pl.BlockSpec((None, None, A_HALF, A_OUT),
                            lambda b, r, n: (b, r, jnp.minimum((n + 1) * halo, n_steps * halo - 1), col))

    return pl.pallas_call(
        functools.partial(_attn_a_kernel, dilation=d, slopes=slopes, n_steps=n_steps),
        grid=(batch, d, n_steps),
        in_specs=[own(0), prev(1), own(1), nxt(1), prev(2), own(2), nxt(2)],
        out_specs=[own(0), own(0)],
        out_shape=[jax.ShapeDtypeStruct((batch, d, length, A_OUT), _BF16),
                   jax.ShapeDtypeStruct((batch, d, length, A_OUT), _F32)],
        scratch_shapes=[pltpu.VMEM((3, A_SLOTS * ATTN_BQ, ATTN_BQ + 2 * A_HALF), _F32)],
        compiler_params=pltpu.CompilerParams(dimension_semantics=("arbitrary",) * 3),
        name=f"attn_a{group}",
    )(qkv, qkv, qkv, qkv, qkv, qkv, qkv)


def _attn_b(qkvb, sink):
    batch, seq, _ = qkvb.shape
    rows = ATTN_BQ * ATTN_SUB
    n_steps = seq // rows
    halo = rows // B_HALF_WINDOW
    slopes = [float(2.0 ** (-8.0 * (h + 1) / B_Q_HEADS)) for h in range(B_Q_HEADS)]
    k_col = B_OUT // B_KV

    def own(col):
        return pl.BlockSpec((None, rows, B_KV), lambda b, n: (b, n, col))

    def prev(col):
        return pl.BlockSpec((None, B_HALF_WINDOW, B_KV),
                            lambda b, n: (b, jnp.maximum(n * halo - 1, 0), col))

    def nxt(col):
        return pl.BlockSpec((None, B_HALF_WINDOW, B_KV),
                            lambda b, n: (b, jnp.minimum((n + 1) * halo, n_steps * halo - 1), col))

    q_spec = pl.BlockSpec((None, rows, B_OUT), lambda b, n: (b, n, 0))
    return pl.pallas_call(
        functools.partial(_attn_b_kernel, slopes=slopes, n_steps=n_steps),
        grid=(batch, n_steps),
        in_specs=[q_spec, prev(k_col), own(k_col), nxt(k_col),
                  prev(k_col + 1), own(k_col + 1), nxt(k_col + 1),
                  pl.BlockSpec(memory_space=pltpu.SMEM)],
        out_specs=q_spec,
        out_shape=jax.ShapeDtypeStruct((batch, seq, B_OUT), _BF16),
        scratch_shapes=[pltpu.VMEM((3, B_Q_HEADS * ATTN_BQ, ATTN_BQ + 2 * B_HALF_WINDOW), _F32)],
        compiler_params=pltpu.CompilerParams(dimension_semantics=("arbitrary",) * 2),
        name="attn_b",
    )(qkvb, qkvb, qkvb, qkvb, qkvb, qkvb, qkvb, sink)


def _mix_ffn(h, oa, la, yb, p):
    batch, seq, _ = h.shape
    tm = TOKEN_TILE
    row = lambda w: pl.BlockSpec((None, tm, w), lambda b, i: (b, i, 0))
    consts = (p["gmix"], p["wgate"], p["bgate"], p["wpa"], p["wpb"], p["wout"],
              p["g2"], p["wg2"], p["wu2"], p["wd2"], p["gfin"])
    attn_specs, attn_args = [], []
    for g, (_, d) in enumerate(A_PAIRS):
        attn_specs += [_residue_spec(tm, d, A_OUT)] * 2
        attn_args += [oa[g], la[g]]
    return pl.pallas_call(
        _mix_ffn_kernel,
        grid=(batch, seq // tm),
        in_specs=[row(D_MODEL)] + attn_specs + [row(B_OUT)] + [_resident(c.shape) for c in consts],
        out_specs=row(D_MODEL),
        out_shape=jax.ShapeDtypeStruct((batch, seq, D_MODEL), _F32),
        scratch_shapes=[pltpu.VMEM((tm, D_MODEL), _F32),
                        pltpu.VMEM((2 * N_GROUPS * A_OUT // LANES, tm, LANES), _F32)],
        compiler_params=pltpu.CompilerParams(
            dimension_semantics=("arbitrary",) * 2, vmem_limit_bytes=V7X_VMEM_LIMIT_BYTES),
        name="mix_ffn",
    )(h, *attn_args, yb, *consts)


def _b_head_perm():
    idx = [(kv * B_GROUP + c) * HEAD_DIM + e
           for c in range(B_GROUP) for kv in range(B_KV_HEADS) for e in range(HEAD_DIM)]
    return np.asarray(idx, dtype=np.int32)


def _prepare_params(ffn1_norm, ffn1_w_gate, ffn1_w_up, ffn1_w_down, mix_norm, w_in, a_q_norm,
                    a_k_norm, b_q_norm, b_k_norm, b_sink, w_proj_a, w_proj_b, w_gate, b_gate, w_out,
                    ffn2_norm, ffn2_w_gate, ffn2_w_up, ffn2_w_down, final_norm):
    def vec(g):
        return g.reshape(1, -1).astype(_F32)

    def chunk_cols(w):
        return w.astype(_BF16).reshape(D_MODEL, N_FF_CHUNKS, FF_CHUNK).transpose(1, 0, 2)

    def chunk_rows(w):
        return w.astype(_BF16).reshape(N_FF_CHUNKS, FF_CHUNK, D_MODEL)

    perm = _b_head_perm()
    qa, ka, va, qb, kb, vb = jnp.split(w_in, np.cumsum([A_W, A_W, A_W, B_OUT, B_KV]).tolist(), axis=1)
    grp = lambda w, g: w[:, g * A_OUT:(g + 1) * A_OUT]
    norm_cols = [m for g in range(N_GROUPS) for m in (grp(qa, g), grp(ka, g))] + [qb[:, perm], kb]
    win = jnp.concatenate(norm_cols + [va, vb], axis=1).astype(_BF16)
    scale = HEAD_DIM ** -0.5 * LOG2_E
    gain = jnp.concatenate(
        [jnp.tile(g, A_SLOTS) for _ in range(N_GROUPS) for g in (a_q_norm * scale, a_k_norm)]
        + [jnp.tile(b_q_norm * scale, B_Q_HEADS), jnp.tile(b_k_norm, B_KV_HEADS)])
    head_of_col = np.arange(N_NORM) // HEAD_DIM
    member = (head_of_col[:, None] == np.arange(HEAD_LANES)[None, :])
    e = jnp.asarray(member, dtype=_BF16)
    et = jnp.asarray(np.concatenate([member.T, member.T], axis=0), dtype=_BF16)
    return dict(
        g1=vec(ffn1_norm), wg1=chunk_cols(ffn1_w_gate), wu1=chunk_cols(ffn1_w_up), wd1=chunk_rows(ffn1_w_down),
        gmix=vec(mix_norm), win=win, e=e, et=et, gain=vec(gain), sink=b_sink.astype(_F32) * LOG2_E,
        wgate=w_gate.astype(_BF16), bgate=vec(b_gate), wpa=w_proj_a.astype(_BF16),
        wpb=w_proj_b[perm, :].astype(_BF16), wout=w_out.astype(_BF16),
        g2=vec(ffn2_norm), wg2=chunk_cols(ffn2_w_gate), wu2=chunk_cols(ffn2_w_up), wd2=chunk_rows(ffn2_w_down),
        gfin=vec(final_norm),
    )


def _encoder(x, p):
    h, a0, a1, a2, qkvb = _ffn_proj(x, p)
    oa, la = zip(*[_attn_a(a, g) for g, a in enumerate((a0, a1, a2))])
    yb = _attn_b(qkvb, p["sink"])
    return _mix_ffn(h, oa, la, yb, p)


def kernel(x_prompt, x_sample, ffn1_norm, ffn1_w_gate, ffn1_w_up, ffn1_w_down, mix_norm, w_in, a_q_norm, a_k_norm, b_q_norm, b_k_norm, b_sink, w_proj_a, w_proj_b, w_gate, b_gate, w_out, ffn2_norm, ffn2_w_gate, ffn2_w_up, ffn2_w_down, final_norm):
    p = _prepare_params(*[w[0] for w in (
        ffn1_norm, ffn1_w_gate, ffn1_w_up, ffn1_w_down, mix_norm, w_in, a_q_norm, a_k_norm, b_q_norm,
        b_k_norm, b_sink, w_proj_a, w_proj_b, w_gate, b_gate, w_out, ffn2_norm, ffn2_w_gate, ffn2_w_up,
        ffn2_w_down, final_norm)])
    return (_encoder(x_prompt, p), _encoder(x_sample, p))
```

```python
import functools

import numpy as np
import jax
import jax.numpy as jnp
from jax import lax
from jax.experimental import pallas as pl
from jax.experimental.pallas import tpu as pltpu

D_MODEL = 1024
D_FF = 2816
HEAD_DIM = 64
A_PAIRS = ((128, 1), (512, 4), (2048, 16))
N_GROUPS = len(A_PAIRS)
A_SLOTS = 4
A_HEADS = A_SLOTS * N_GROUPS
A_OUT = A_SLOTS * HEAD_DIM
A_W = A_HEADS * HEAD_DIM
B_Q_HEADS = 8
B_KV_HEADS = 2
B_GROUP = B_Q_HEADS // B_KV_HEADS
B_HALF_WINDOW = 128
B_OUT = B_Q_HEADS * HEAD_DIM
B_KV = B_KV_HEADS * HEAD_DIM
NORM_EPS = 1e-6
MASK_VALUE = -1e30
LOG2_E = 1.4426950408889634
LN_2 = 0.6931471805599453

HEAD_LANES = 128
GROUP_W = 3 * A_OUT
QKVB_W = B_OUT + 2 * B_KV

LANES = 128
FF_CHUNK = 256
N_FF_CHUNKS = D_FF // FF_CHUNK
TOKEN_TILE = 512
ATTN_BQ = 128
ATTN_STEP_ROWS = 1024
A_HALF = 64
V7X_VMEM_LIMIT_BYTES = 56 * 1024 * 1024

_F32 = jnp.float32
_BF16 = jnp.bfloat16


def _rms(x, g):
    ms = jnp.mean(x * x, axis=-1, keepdims=True)
    return x * lax.rsqrt(ms + NORM_EPS) * g


def _sigmoid(x):
    return 0.5 * jnp.tanh(0.5 * x) + 0.5


def _swiglu_into(acc_ref, xn, wg_ref, wu_ref, wd_ref):
    for c in range(N_FF_CHUNKS):
        cols = slice(c * FF_CHUNK, (c + 1) * FF_CHUNK)
        g = jnp.dot(xn, wg_ref[:, cols], preferred_element_type=_F32)
        u = jnp.dot(xn, wu_ref[:, cols], preferred_element_type=_F32)
        a = (g * _sigmoid(g) * u).astype(_BF16)
        part = jnp.dot(a, wd_ref[cols, :], preferred_element_type=_F32)
        if c == 0:
            acc_ref[...] = part
        else:
            acc_ref[...] += part


def _qk_normed(chunk, n_norm, e_ref, et_ref, gain):
    nrm = chunk[:, :n_norm]
    ssq = jnp.dot((nrm * nrm).astype(_BF16), e_ref[:n_norm, :], preferred_element_type=_F32)
    inv = lax.rsqrt(ssq * (1.0 / HEAD_DIM) + NORM_EPS)
    inv_hi = inv.astype(_BF16)
    inv_lo = (inv - inv_hi.astype(_F32)).astype(_BF16)
    inv_b = jnp.dot(jnp.concatenate([inv_hi, inv_lo], axis=1), et_ref[:, :n_norm],
                    preferred_element_type=_F32)
    return nrm * inv_b * gain, chunk[:, n_norm:]


def _ffn_proj_kernel(x_ref, g1_ref, wg_ref, wu_ref, wd_ref, gmix_ref, win_ref, e_ref, et_ref,
                     gain_ref, h_ref, a0_ref, a1_ref, a2_ref, qkvb_ref, acc_ref, slab_ref):
    tm = x_ref.shape[0]
    x = x_ref[...]
    xn = _rms(x, g1_ref[...]).astype(_BF16)
    _swiglu_into(acc_ref, xn, wg_ref, wu_ref, wd_ref)
    h = x + 0.5 * acc_ref[...]
    h_ref[...] = h
    u = _rms(h, gmix_ref[...]).astype(_BF16)
    n_slab = GROUP_W // LANES
    for g, a_ref in ((2, a2_ref), (1, a1_ref), (0, a0_ref)):
        lo = g * GROUP_W
        chunk = jnp.dot(u, win_ref[:, lo:lo + GROUP_W], preferred_element_type=_F32)
        nq, val = _qk_normed(chunk, 2 * A_OUT, e_ref, et_ref, gain_ref[:, lo:lo + 2 * A_OUT])
        d = A_PAIRS[g][1]
        if d == 1:
            a_ref[0, :, :2 * A_OUT] = nq.astype(_BF16)
            a_ref[0, :, 2 * A_OUT:] = val.astype(_BF16)
            continue
        base = (g - 1) * n_slab
        for s in range(n_slab):
            src = nq if s * LANES < 2 * A_OUT else val
            col = s * LANES % (2 * A_OUT)
            slab_ref[base + s] = src[:, col:col + LANES]
        for r in range(d):
            for s in range(n_slab):
                a_ref[r, :, s * LANES:(s + 1) * LANES] = (
                    slab_ref[base + s, pl.ds(r, tm // d, stride=d), :].astype(_BF16))
    lo = N_GROUPS * GROUP_W
    chunk = jnp.dot(u, win_ref[:, lo:lo + QKVB_W], preferred_element_type=_F32)
    nq, val = _qk_normed(chunk, B_OUT + B_KV, e_ref, et_ref, gain_ref[:, lo:lo + B_OUT + B_KV])
    qkvb_ref[:, :B_OUT + B_KV] = nq.astype(_BF16)
    qkvb_ref[:, B_OUT + B_KV:] = val.astype(_BF16)


def _band_bias(rows_per_head, n_heads, bk, half, slopes, dist_scale, key_lo, key_hi):
    shape = (n_heads * rows_per_head, bk)
    row = lax.broadcasted_iota(jnp.int32, shape, 0)
    col = lax.broadcasted_iota(jnp.int32, shape, 1)
    rel = jnp.abs(col - half - (row % rows_per_head))
    head = row // rows_per_head
    slope = jnp.full(shape, slopes[0], _F32)
    for h in range(1, n_heads):
        slope = jnp.where(head == h, slopes[h], slope)
    dist = (rel * dist_scale).astype(_F32)
    ok = (rel <= half) & (col >= key_lo) & (col < key_hi)
    return jnp.where(ok, -slope * LOG2_E * dist, MASK_VALUE)


def _init_bias_tables(bias_ref, rows_per_head, n_heads, half, slopes, dist_scale):
    bk = rows_per_head + 2 * half
    for t, (lo, hi) in enumerate(((0, bk), (half, bk), (0, rows_per_head + half))):
        bias_ref[t] = _band_bias(rows_per_head, n_heads, bk, half, slopes, dist_scale, lo, hi)


def _table_index(j, n_sub, step, n_steps):
    if j == 0:
        return jnp.where(step == 0, 1, 0)
    if j == n_sub - 1:
        return jnp.where(step == n_steps - 1, 2, 0)
    return 0


def _attn_a_kernel(q_ref, kp_ref, ko_ref, kn_ref, vp_ref, vo_ref, vn_ref, o_ref, lse_ref, bias_ref,
                   *, dilation, slopes, n_steps):
    step = pl.program_id(2)
    bq = ATTN_BQ
    bk = bq + 2 * A_HALF
    planes, rows, _ = q_ref.shape
    n_sub = rows // bq

    @pl.when((pl.program_id(0) == 0) & (pl.program_id(1) == 0) & (step == 0))
    def _():
        _init_bias_tables(bias_ref, bq, A_SLOTS, A_HALF, slopes, dilation)

    lane_head = lax.broadcasted_iota(jnp.int32, (bq, A_OUT), 1) // HEAD_DIM
    for r in range(planes):
        k_all = jnp.concatenate([kp_ref[r], ko_ref[r], kn_ref[r]], axis=0)
        v_all = jnp.concatenate([vp_ref[r], vo_ref[r], vn_ref[r]], axis=0)
        for j in range(n_sub):
            q = q_ref[r, j * bq:(j + 1) * bq, :]
            zero = jnp.zeros_like(q)
            qs = jnp.concatenate([jnp.where(lane_head == h, q, zero) for h in range(A_SLOTS)], axis=0)
            k = k_all[j * bq:j * bq + bk]
            v = v_all[j * bq:j * bq + bk]
            s = lax.dot_general(qs, k, (((1,), (1,)), ((), ())), preferred_element_type=_F32)
            s = s + bias_ref[_table_index(j, n_sub, step, n_steps)]
            m = jnp.max(s, axis=1, keepdims=True)
            p = jnp.exp2(s - m)
            l = jnp.sum(p, axis=1, keepdims=True)
            pv = jnp.dot(p.astype(_BF16), v, preferred_element_type=_F32)
            on = pv * (1.0 / l)
            lse = (m + jnp.log2(l)) * LN_2
            o = on[:bq]
            lse_t = jnp.broadcast_to(lse[:bq], (bq, A_OUT))
            for h in range(1, A_SLOTS):
                sel = lane_head == h
                o = jnp.where(sel, on[h * bq:(h + 1) * bq], o)
                lse_t = jnp.where(sel, jnp.broadcast_to(lse[h * bq:(h + 1) * bq], (bq, A_OUT)), lse_t)
            o_ref[r, j * bq:(j + 1) * bq, :] = o.astype(o_ref.dtype)
            lse_ref[r, j * bq:(j + 1) * bq, :] = lse_t


def _attn_b_kernel(q_ref, kp_ref, ko_ref, kn_ref, vp_ref, vo_ref, vn_ref, sink_ref, o_ref, bias_ref,
                   *, slopes, n_steps):
    step = pl.program_id(1)
    bq = ATTN_BQ
    bk = bq + 2 * B_HALF_WINDOW
    n_sub = q_ref.shape[0] // bq

    @pl.when((pl.program_id(0) == 0) & (step == 0))
    def _():
        _init_bias_tables(bias_ref, bq, B_Q_HEADS, B_HALF_WINDOW, slopes, 1)

    k_all = jnp.concatenate([kp_ref[...], ko_ref[...], kn_ref[...]], axis=0)
    v_all = jnp.concatenate([vp_ref[...], vo_ref[...], vn_ref[...]], axis=0)
    v_all = jnp.concatenate([v_all, jnp.ones_like(v_all)], axis=1)
    lane_kv = lax.broadcasted_iota(jnp.int32, (bq, B_KV), 1) // HEAD_DIM
    for j in range(n_sub):
        parts = []
        for kv in range(B_KV_HEADS):
            for c in range(B_GROUP):
                chunk = q_ref[j * bq:(j + 1) * bq, c * B_KV:(c + 1) * B_KV]
                parts.append(jnp.where(lane_kv == kv, chunk, jnp.zeros_like(chunk)))
        qs = jnp.concatenate(parts, axis=0)
        k = k_all[j * bq:j * bq + bk]
        v = v_all[j * bq:j * bq + bk]
        s = lax.dot_general(qs, k, (((1,), (1,)), ((), ())), preferred_element_type=_F32)
        s = s + bias_ref[_table_index(j, n_sub, step, n_steps)]
        ms, ps = [], []
        for h in range(B_Q_HEADS):
            s_h = s[h * bq:(h + 1) * bq]
            m_h = jnp.maximum(jnp.max(s_h, axis=1, keepdims=True), sink_ref[h])
            ps.append(jnp.exp2(s_h - m_h).astype(_BF16))
            ms.append(m_h)
        pv = jnp.dot(jnp.concatenate(ps, axis=0), v, preferred_element_type=_F32)

        def head_out(h):
            rows = pv[h * bq:(h + 1) * bq]
            return rows[:, :B_KV] / (rows[:, B_KV:] + jnp.exp2(sink_ref[h] - ms[h]))

        for c in range(B_GROUP):
            o_ref[j * bq:(j + 1) * bq, c * B_KV:(c + 1) * B_KV] = jnp.where(
                lane_kv == 0, head_out(c), head_out(B_GROUP + c)).astype(o_ref.dtype)


def _natural_order(ref, dilation, slab_ref, base):
    n_slab = A_OUT // LANES
    if dilation == 1:
        x = ref[0].astype(_F32)
        return [x[:, s * LANES:(s + 1) * LANES] for s in range(n_slab)]
    rows = ref.shape[1]
    for r in range(dilation):
        blk = ref[r].astype(_F32)
        for s in range(n_slab):
            slab_ref[base + s, pl.ds(r, rows, stride=dilation), :] = blk[:, s * LANES:(s + 1) * LANES]
    return [slab_ref[base + s] for s in range(n_slab)]


def _mix_ffn_kernel(h_ref, oa0_ref, la0_ref, oa1_ref, la1_ref, oa2_ref, la2_ref, yb_ref,
                    gmix_ref, wgate_ref, bgate_ref, wpa_ref, wpb_ref, wout_ref,
                    g2_ref, wg_ref, wu_ref, wd_ref, gfin_ref, y_ref, acc_ref, slab_ref):
    n_slab = A_OUT // LANES
    outs, lses = [], []
    for g, (o_ref, l_ref) in enumerate(((oa0_ref, la0_ref), (oa1_ref, la1_ref), (oa2_ref, la2_ref))):
        d = A_PAIRS[g][1]
        outs.append(_natural_order(o_ref, d, slab_ref, 2 * g * n_slab))
        lses.append(_natural_order(l_ref, d, slab_ref, (2 * g + 1) * n_slab))
    ya = []
    for s in range(n_slab):
        la0, la1, la2 = lses[0][s], lses[1][s], lses[2][s]
        mx = jnp.maximum(jnp.maximum(la0, la1), la2)
        e0, e1, e2 = jnp.exp(la0 - mx), jnp.exp(la1 - mx), jnp.exp(la2 - mx)
        ya.append((e0 * outs[0][s] + e1 * outs[1][s] + e2 * outs[2][s]) / (e0 + e1 + e2))
    ya = jnp.concatenate(ya, axis=1)
    h = h_ref[...]
    u = _rms(h, gmix_ref[...]).astype(_BF16)
    gates = _sigmoid(jnp.dot(u, wgate_ref[...], preferred_element_type=_F32) + bgate_ref[...])
    pa = jnp.dot(ya.astype(_BF16), wpa_ref[...], preferred_element_type=_F32)
    pb = jnp.dot(yb_ref[...], wpb_ref[...], preferred_element_type=_F32)
    mixed = gates[:, :D_MODEL] * pa + gates[:, D_MODEL:] * pb
    h = h + jnp.dot(mixed.astype(_BF16), wout_ref[...], preferred_element_type=_F32)
    hn = _rms(h, g2_ref[...]).astype(_BF16)
    _swiglu_into(acc_ref, hn, wg_ref, wu_ref, wd_ref)
    h = h + 0.5 * acc_ref[...]
    y_ref[...] = _rms(h, gfin_ref[...])


def _resident(shape):
    nd = len(shape)
    return pl.BlockSpec(shape, lambda *_: (0,) * nd, pipeline_mode=pl.Buffered(1))


def _residue_spec(tm, d, width):
    return pl.BlockSpec((None, d, tm // d, width), lambda b, i: (b, 0, i, 0))


def _ffn_proj(x, p):
    batch, seq, _ = x.shape
    tm = TOKEN_TILE
    row = lambda w: pl.BlockSpec((None, tm, w), lambda b, i: (b, i, 0))
    consts = (p["g1"], p["wg1"], p["wu1"], p["wd1"], p["gmix"], p["win"], p["e"], p["et"], p["gain"])
    dils = [d for _, d in A_PAIRS]
    return pl.pallas_call(
        _ffn_proj_kernel,
        grid=(batch, seq // tm),
        in_specs=[row(D_MODEL)] + [_resident(c.shape) for c in consts],
        out_specs=[row(D_MODEL)] + [_residue_spec(tm, d, GROUP_W) for d in dils] + [row(QKVB_W)],
        out_shape=[jax.ShapeDtypeStruct((batch, seq, D_MODEL), _F32)]
        + [jax.ShapeDtypeStruct((batch, d, seq // d, GROUP_W), _BF16) for d in dils]
        + [jax.ShapeDtypeStruct((batch, seq, QKVB_W), _BF16)],
        scratch_shapes=[pltpu.VMEM((tm, D_MODEL), _F32),
                        pltpu.VMEM((2 * GROUP_W // LANES, tm, LANES), _F32)],
        compiler_params=pltpu.CompilerParams(
            dimension_semantics=("arbitrary",) * 2, vmem_limit_bytes=V7X_VMEM_LIMIT_BYTES),
        name="ffn_proj",
    )(x, *consts)


def _attn_a(qkv, group):
    batch, d, length, _ = qkv.shape
    rows = min(length, ATTN_STEP_ROWS)
    planes = ATTN_STEP_ROWS // rows
    n_steps = length // rows
    halo = rows // A_HALF
    slopes = [float(2.0 ** (-8.0 * (group * A_SLOTS + h + 1) / A_HEADS)) for h in range(A_SLOTS)]

    def own(col):
        return pl.BlockSpec((None, planes, rows, A_OUT), lambda b, r, n: (b, r, n, col))

    def prev(col):
        return pl.BlockSpec((None, planes, A_HALF, A_OUT),
                            lambda b, r, n: (b, r, jnp.maximum(n * halo - 1, 0), col))

    def nxt(col):
        return pl.BlockSpec((None, planes, A_HALF, A_OUT),
                            lambda b, r, n: (b, r, jnp.minimum((n + 1) * halo, n_steps * halo - 1), col))

    return pl.pallas_call(
        functools.partial(_attn_a_kernel, dilation=d, slopes=slopes, n_steps=n_steps),
        grid=(batch, d // planes, n_steps),
        in_specs=[own(0), prev(1), own(1), nxt(1), prev(2), own(2), nxt(2)],
        out_specs=[own(0), own(0)],
        out_shape=[jax.ShapeDtypeStruct((batch, d, length, A_OUT), _BF16),
                   jax.ShapeDtypeStruct((batch, d, length, A_OUT), _F32)],
        scratch_shapes=[pltpu.VMEM((3, A_SLOTS * ATTN_BQ, ATTN_BQ + 2 * A_HALF), _F32)],
        compiler_params=pltpu.CompilerParams(dimension_semantics=("arbitrary",) * 3),
        name=f"attn_a{group}",
    )(qkv, qkv, qkv, qkv, qkv, qkv, qkv)


def _attn_b(qkvb, sink):
    batch, seq, _ = qkvb.shape
    rows = ATTN_STEP_ROWS
    n_steps = seq // rows
    halo = rows // B_HALF_WINDOW
    slopes = [float(2.0 ** (-8.0 * (h + 1) / B_Q_HEADS)) for h in range(B_Q_HEADS)]
    k_col = B_OUT // B_KV

    def own(col):
        return pl.BlockSpec((None, rows, B_KV), lambda b, n: (b, n, col))

    def prev(col):
        return pl.BlockSpec((None, B_HALF_WINDOW, B_KV),
                            lambda b, n: (b, jnp.maximum(n * halo - 1, 0), col))

    def nxt(col):
        return pl.BlockSpec((None, B_HALF_WINDOW, B_KV),
                            lambda b, n: (b, jnp.minimum((n + 1) * halo, n_steps * halo - 1), col))

    q_spec = pl.BlockSpec((None, rows, B_OUT), lambda b, n: (b, n, 0))
    return pl.pallas_call(
        functools.partial(_attn_b_kernel, slopes=slopes, n_steps=n_steps),
        grid=(batch, n_steps),
        in_specs=[q_spec, prev(k_col), own(k_col), nxt(k_col),
                  prev(k_col + 1), own(k_col + 1), nxt(k_col + 1),
                  pl.BlockSpec(memory_space=pltpu.SMEM)],
        out_specs=q_spec,
        out_shape=jax.ShapeDtypeStruct((batch, seq, B_OUT), _BF16),
        scratch_shapes=[pltpu.VMEM((3, B_Q_HEADS * ATTN_BQ, ATTN_BQ + 2 * B_HALF_WINDOW), _F32)],
        compiler_params=pltpu.CompilerParams(dimension_semantics=("arbitrary",) * 2),
        name="attn_b",
    )(qkvb, qkvb, qkvb, qkvb, qkvb, qkvb, qkvb, sink)


def _mix_ffn(h, oa, la, yb, p):
    batch, seq, _ = h.shape
    tm = TOKEN_TILE
    row = lambda w: pl.BlockSpec((None, tm, w), lambda b, i: (b, i, 0))
    consts = (p["gmix"], p["wgate"], p["bgate"], p["wpa"], p["wpb"], p["wout"],
              p["g2"], p["wg2"], p["wu2"], p["wd2"], p["gfin"])
    attn_specs, attn_args = [], []
    for g, (_, d) in enumerate(A_PAIRS):
        attn_specs += [_residue_spec(tm, d, A_OUT)] * 2
        attn_args += [oa[g], la[g]]
    return pl.pallas_call(
        _mix_ffn_kernel,
        grid=(batch, seq // tm),
        in_specs=[row(D_MODEL)] + attn_specs + [row(B_OUT)] + [_resident(c.shape) for c in consts],
        out_specs=row(D_MODEL),
        out_shape=jax.ShapeDtypeStruct((batch, seq, D_MODEL), _F32),
        scratch_shapes=[pltpu.VMEM((tm, D_MODEL), _F32),
                        pltpu.VMEM((2 * N_GROUPS * A_OUT // LANES, tm, LANES), _F32)],
        compiler_params=pltpu.CompilerParams(
            dimension_semantics=("arbitrary",) * 2, vmem_limit_bytes=V7X_VMEM_LIMIT_BYTES),
        name="mix_ffn",
    )(h, *attn_args, yb, *consts)


def _b_head_perm():
    idx = [(kv * B_GROUP + c) * HEAD_DIM + e
           for c in range(B_GROUP) for kv in range(B_KV_HEADS) for e in range(HEAD_DIM)]
    return np.asarray(idx, dtype=np.int32)


def _prepare_params(ffn1_norm, ffn1_w_gate, ffn1_w_up, ffn1_w_down, mix_norm, w_in, a_q_norm,
                    a_k_norm, b_q_norm, b_k_norm, b_sink, w_proj_a, w_proj_b, w_gate, b_gate, w_out,
                    ffn2_norm, ffn2_w_gate, ffn2_w_up, ffn2_w_down, final_norm):
    def vec(g):
        return g.reshape(1, -1).astype(_F32)

    def bf16(w):
        return w.astype(_BF16)

    perm = _b_head_perm()
    qa, ka, va, qb, kb, vb = jnp.split(w_in, np.cumsum([A_W, A_W, A_W, B_OUT, B_KV]).tolist(), axis=1)
    grp = lambda w, g: w[:, g * A_OUT:(g + 1) * A_OUT]
    chunks = [m for g in range(N_GROUPS) for m in (grp(qa, g), grp(ka, g), grp(va, g))] + [qb[:, perm], kb, vb]
    win = jnp.concatenate(chunks, axis=1).astype(_BF16)
    scale = HEAD_DIM ** -0.5 * LOG2_E
    ones = lambda n: jnp.ones((n,), _F32)
    gain = jnp.concatenate(
        [jnp.tile(a_q_norm * scale, A_SLOTS), jnp.tile(a_k_norm, A_SLOTS), ones(A_OUT)] * N_GROUPS
        + [jnp.tile(b_q_norm * scale, B_Q_HEADS), jnp.tile(b_k_norm, B_KV_HEADS), ones(B_KV)])
    head_of_col = np.arange(B_OUT + B_KV) // HEAD_DIM
    member = (head_of_col[:, None] == np.arange(HEAD_LANES)[None, :])
    e = jnp.asarray(member, dtype=_BF16)
    et = jnp.asarray(np.concatenate([member.T, member.T], axis=0), dtype=_BF16)
    return dict(
        g1=vec(ffn1_norm), wg1=bf16(ffn1_w_gate), wu1=bf16(ffn1_w_up), wd1=bf16(ffn1_w_down),
        gmix=vec(mix_norm), win=win, e=e, et=et, gain=vec(gain), sink=b_sink.astype(_F32) * LOG2_E,
        wgate=w_gate.astype(_BF16), bgate=vec(b_gate), wpa=w_proj_a.astype(_BF16),
        wpb=w_proj_b[perm, :].astype(_BF16), wout=w_out.astype(_BF16),
        g2=vec(ffn2_norm), wg2=bf16(ffn2_w_gate), wu2=bf16(ffn2_w_up), wd2=bf16(ffn2_w_down),
        gfin=vec(final_norm),
    )


def _encoder(x, p):
    h, a0, a1, a2, qkvb = _ffn_proj(x, p)
    oa, la = zip(*[_attn_a(a, g) for g, a in enumerate((a0, a1, a2))])
    yb = _attn_b(qkvb, p["sink"])
    return _mix_ffn(h, oa, la, yb, p)


def kernel(x_prompt, x_sample, ffn1_norm, ffn1_w_gate, ffn1_w_up, ffn1_w_down, mix_norm, w_in, a_q_norm, a_k_norm, b_q_norm, b_k_norm, b_sink, w_proj_a, w_proj_b, w_gate, b_gate, w_out, ffn2_norm, ffn2_w_gate, ffn2_w_up, ffn2_w_down, final_norm):
    p = _prepare_params(*[w[0] for w in (
        ffn1_norm, ffn1_w_gate, ffn1_w_up, ffn1_w_down, mix_norm, w_in, a_q_norm, a_k_norm, b_q_norm,
        b_k_norm, b_sink, w_proj_a, w_proj_b, w_gate, b_gate, w_out, ffn2_norm, ffn2_w_gate, ffn2_w_up,
        ffn2_w_down, final_norm)])
    return (_encoder(x_prompt, p), _encoder(x_sample, p))
```

```python
import functools

import numpy as np
import jax
import jax.numpy as jnp
from jax import lax
from jax.experimental import pallas as pl
from jax.experimental.pallas import tpu as pltpu

D_MODEL = 1024
D_FF = 2816
HEAD_DIM = 64
A_PAIRS = ((128, 1), (512, 4), (2048, 16))
N_GROUPS = len(A_PAIRS)
A_SLOTS = 4
A_HEADS = A_SLOTS * N_GROUPS
A_OUT = A_SLOTS * HEAD_DIM
A_W = A_HEADS * HEAD_DIM
B_Q_HEADS = 8
B_KV_HEADS = 2
B_GROUP = B_Q_HEADS // B_KV_HEADS
B_HALF_WINDOW = 128
B_OUT = B_Q_HEADS * HEAD_DIM
B_KV = B_KV_HEADS * HEAD_DIM
NORM_EPS = 1e-6
MASK_VALUE = -1e30
LOG2_E = 1.4426950408889634
LN_2 = 0.6931471805599453

MXU_TILE = 256
GROUP_W = 3 * A_OUT
QKVB_W = B_OUT + 2 * B_KV

LANES = 128
FF_CHUNKS = (512,) * 5 + (256,)
TOKEN_TILE = 512
ATTN_BQ = 128
ATTN_STEP_ROWS = 1024
A_HALF = 64
V7X_VMEM_LIMIT_BYTES = 56 * 1024 * 1024

_F32 = jnp.float32
_BF16 = jnp.bfloat16


def _rms(x, g):
    ms = jnp.mean(x * x, axis=-1, keepdims=True)
    return x * lax.rsqrt(ms + NORM_EPS) * g


def _sigmoid(x):
    return 0.5 * jnp.tanh(0.5 * x) + 0.5


def _swiglu_into(acc_ref, xn, wgu_ref, wd_ref):
    lo = 0
    for w in FF_CHUNKS:
        gu = jnp.dot(xn, wgu_ref[:, 2 * lo:2 * (lo + w)], preferred_element_type=_F32)
        g, u = gu[:, :w], gu[:, w:]
        a = (g * _sigmoid(g) * u).astype(_BF16)
        part = jnp.dot(a, wd_ref[lo:lo + w, :], preferred_element_type=_F32)
        if lo == 0:
            acc_ref[...] = part
        else:
            acc_ref[...] += part
        lo += w


def _qk_normed(chunk, n_norm, same_head_ref, gain):
    parts = []
    for lo in range(0, n_norm, MXU_TILE):
        w = min(MXU_TILE, n_norm - lo)
        x = chunk[:, lo:lo + w]
        ssq = jnp.dot((x * x).astype(_BF16), same_head_ref[:w, :w], preferred_element_type=_F32)
        parts.append(x * lax.rsqrt(ssq * (1.0 / HEAD_DIM) + NORM_EPS))
    return jnp.concatenate(parts, axis=1) * gain, chunk[:, n_norm:]


def _ffn_proj_kernel(x_ref, g1_ref, wgu_ref, wd_ref, gmix_ref, win_ref, same_head_ref,
                     gain_ref, h_ref, a0_ref, a1_ref, a2_ref, qkvb_ref, acc_ref, slab_ref):
    tm = x_ref.shape[0]
    x = x_ref[...]
    xn = _rms(x, g1_ref[...]).astype(_BF16)
    _swiglu_into(acc_ref, xn, wgu_ref, wd_ref)
    h = x + 0.5 * acc_ref[...]
    h_ref[...] = h
    u = _rms(h, gmix_ref[...]).astype(_BF16)
    n_slab = GROUP_W // LANES

    def project_group(g, a_ref):
        lo = g * GROUP_W
        chunk = jnp.dot(u, win_ref[:, lo:lo + GROUP_W], preferred_element_type=_F32)
        nq, val = _qk_normed(chunk, 2 * A_OUT, same_head_ref, gain_ref[:, lo:lo + 2 * A_OUT])
        d = A_PAIRS[g][1]
        if d == 1:
            a_ref[0, :, :2 * A_OUT] = nq.astype(_BF16)
            a_ref[0, :, 2 * A_OUT:] = val.astype(_BF16)
            return
        base = (g - 1) * n_slab
        for s in range(n_slab):
            src = nq if s * LANES < 2 * A_OUT else val
            col = s * LANES % (2 * A_OUT)
            slab_ref[base + s] = src[:, col:col + LANES]
        for r in range(d):
            for s in range(n_slab):
                a_ref[r, :, s * LANES:(s + 1) * LANES] = (
                    slab_ref[base + s, pl.ds(r, tm // d, stride=d), :].astype(_BF16))

    def project_b():
        lo = N_GROUPS * GROUP_W
        chunk = jnp.dot(u, win_ref[:, lo:lo + QKVB_W], preferred_element_type=_F32)
        nq, val = _qk_normed(chunk, B_OUT + B_KV, same_head_ref, gain_ref[:, lo:lo + B_OUT + B_KV])
        qkvb_ref[:, :B_OUT + B_KV] = nq.astype(_BF16)
        qkvb_ref[:, B_OUT + B_KV:] = val.astype(_BF16)

    project_group(2, a2_ref)
    project_group(1, a1_ref)
    project_group(0, a0_ref)
    project_b()


def _band_bias(rows_per_head, n_heads, bk, half, slopes, dist_scale, key_lo, key_hi):
    shape = (n_heads * rows_per_head, bk)
    row = lax.broadcasted_iota(jnp.int32, shape, 0)
    col = lax.broadcasted_iota(jnp.int32, shape, 1)
    rel = jnp.abs(col - half - (row % rows_per_head))
    head = row // rows_per_head
    slope = jnp.full(shape, slopes[0], _F32)
    for h in range(1, n_heads):
        slope = jnp.where(head == h, slopes[h], slope)
    dist = (rel * dist_scale).astype(_F32)
    ok = (rel <= half) & (col >= key_lo) & (col < key_hi)
    return jnp.where(ok, -slope * LOG2_E * dist, MASK_VALUE)


def _init_bias_tables(bias_ref, rows_per_head, n_heads, half, slopes, dist_scale):
    bk = rows_per_head + 2 * half
    for t, (lo, hi) in enumerate(((0, bk), (half, bk), (0, rows_per_head + half))):
        bias_ref[t] = _band_bias(rows_per_head, n_heads, bk, half, slopes, dist_scale, lo, hi)


def _table_index(j, n_sub, step, n_steps):
    if j == 0:
        return jnp.where(step == 0, 1, 0)
    if j == n_sub - 1:
        return jnp.where(step == n_steps - 1, 2, 0)
    return 0


def _attn_a_kernel(q_ref, kp_ref, ko_ref, kn_ref, vp_ref, vo_ref, vn_ref, o_ref, lse_ref, bias_ref,
                   *, dilation, slopes, n_steps):
    step = pl.program_id(2)
    bq = ATTN_BQ
    bk = bq + 2 * A_HALF
    planes, rows, _ = q_ref.shape
    n_sub = rows // bq

    @pl.when((pl.program_id(0) == 0) & (pl.program_id(1) == 0) & (step == 0))
    def _():
        _init_bias_tables(bias_ref, bq, A_SLOTS, A_HALF, slopes, dilation)

    lane_head = lax.broadcasted_iota(jnp.int32, (bq, A_OUT), 1) // HEAD_DIM
    for r in range(planes):
        k_all = jnp.concatenate([kp_ref[r], ko_ref[r], kn_ref[r]], axis=0)
        v_all = jnp.concatenate([vp_ref[r], vo_ref[r], vn_ref[r]], axis=0)
        for j in range(n_sub):
            q = q_ref[r, j * bq:(j + 1) * bq, :]
            zero = jnp.zeros_like(q)
            qs = jnp.concatenate([jnp.where(lane_head == h, q, zero) for h in range(A_SLOTS)], axis=0)
            k = k_all[j * bq:j * bq + bk]
            v = v_all[j * bq:j * bq + bk]
            s = lax.dot_general(qs, k, (((1,), (1,)), ((), ())), preferred_element_type=_F32)
            s = s + bias_ref[_table_index(j, n_sub, step, n_steps)]
            m = jnp.max(s, axis=1, keepdims=True)
            p = jnp.exp2(s - m)
            l = jnp.sum(p, axis=1, keepdims=True)
            pv = jnp.dot(p.astype(_BF16), v, preferred_element_type=_F32)
            on = pv * (1.0 / l)
            lse = (m + jnp.log2(l)) * LN_2
            o = on[:bq]
            lse_t = jnp.broadcast_to(lse[:bq], (bq, A_OUT))
            for h in range(1, A_SLOTS):
                sel = lane_head == h
                o = jnp.where(sel, on[h * bq:(h + 1) * bq], o)
                lse_t = jnp.where(sel, jnp.broadcast_to(lse[h * bq:(h + 1) * bq], (bq, A_OUT)), lse_t)
            o_ref[r, j * bq:(j + 1) * bq, :] = o.astype(o_ref.dtype)
            lse_ref[r, j * bq:(j + 1) * bq, :] = lse_t


def _init_bias_tables_b(bias_ref, slopes):
    bq, half = ATTN_BQ, B_HALF_WINDOW
    bk = bq + 2 * half
    shape = (B_GROUP * bq, B_KV_HEADS * bk)
    row = lax.broadcasted_iota(jnp.int32, shape, 0)
    col = lax.broadcasted_iota(jnp.int32, shape, 1)
    key = col % bk
    head = (col // bk) * B_GROUP + row // bq
    rel = jnp.abs(key - half - (row % bq))
    slope = jnp.full(shape, slopes[0], _F32)
    for h in range(1, B_Q_HEADS):
        slope = jnp.where(head == h, slopes[h], slope)
    bias = -slope * LOG2_E * rel.astype(_F32)
    for t, (lo, hi) in enumerate(((0, bk), (half, bk), (0, bq + half))):
        bias_ref[t] = jnp.where((rel <= half) & (key >= lo) & (key < hi), bias, MASK_VALUE)


def _attn_b_kernel(q_ref, kp_ref, ko_ref, kn_ref, vp_ref, vo_ref, vn_ref, sink_ref, o_ref, bias_ref,
                   *, slopes, n_steps):
    step = pl.program_id(1)
    bq = ATTN_BQ
    bk = bq + 2 * B_HALF_WINDOW
    n_sub = q_ref.shape[0] // bq

    @pl.when((pl.program_id(0) == 0) & (step == 0))
    def _():
        _init_bias_tables_b(bias_ref, slopes)

    k_all = jnp.concatenate([kp_ref[...], ko_ref[...], kn_ref[...]], axis=0)
    v_all = jnp.concatenate([vp_ref[...], vo_ref[...], vn_ref[...]], axis=0)
    lane_kv = lax.broadcasted_iota(jnp.int32, (bq, B_KV), 1) // HEAD_DIM
    key_lane_kv = lax.broadcasted_iota(jnp.int32, k_all.shape, 1) // HEAD_DIM
    zero = jnp.zeros_like(k_all)
    k_kv = [jnp.where(key_lane_kv == kv, k_all, zero) for kv in range(B_KV_HEADS)]
    v_kv = [jnp.concatenate([jnp.where(key_lane_kv == kv, v_all, zero),
                             jnp.where(key_lane_kv == kv, 1.0, 0.0).astype(_BF16)], axis=1)
            for kv in range(B_KV_HEADS)]
    for j in range(n_sub):
        keys = slice(j * bq, j * bq + bk)
        qs = jnp.concatenate([q_ref[j * bq:(j + 1) * bq, c * B_KV:(c + 1) * B_KV]
                              for c in range(B_GROUP)], axis=0)
        k = jnp.concatenate([k_kv[kv][keys] for kv in range(B_KV_HEADS)], axis=0)
        v = jnp.concatenate([v_kv[kv][keys] for kv in range(B_KV_HEADS)], axis=0)
        s = lax.dot_general(qs, k, (((1,), (1,)), ((), ())), preferred_element_type=_F32)
        s = s + bias_ref[_table_index(j, n_sub, step, n_steps)]
        ps, sink_terms = [], []
        for c in range(B_GROUP):
            p_c, e_c = [], []
            for kv in range(B_KV_HEADS):
                sink = sink_ref[kv * B_GROUP + c]
                s_h = s[c * bq:(c + 1) * bq, kv * bk:(kv + 1) * bk]
                m_h = jnp.maximum(jnp.max(s_h, axis=1, keepdims=True), sink)
                p_c.append(jnp.exp2(s_h - m_h).astype(_BF16))
                e_c.append(jnp.exp2(sink - m_h))
            ps.append(jnp.concatenate(p_c, axis=1))
            sink_terms.append(e_c)
        pv = jnp.dot(jnp.concatenate(ps, axis=0), v, preferred_element_type=_F32)
        for c in range(B_GROUP):
            rows = pv[c * bq:(c + 1) * bq]
            den = rows[:, B_KV:] + jnp.where(lane_kv == 0, sink_terms[c][0], sink_terms[c][1])
            o_ref[j * bq:(j + 1) * bq, c * B_KV:(c + 1) * B_KV] = (rows[:, :B_KV] / den).astype(o_ref.dtype)


def _natural_order(ref, dilation, slab_ref, base):
    n_slab = A_OUT // LANES
    if dilation == 1:
        x = ref[0].astype(_F32)
        return [x[:, s * LANES:(s + 1) * LANES] for s in range(n_slab)]
    rows = ref.shape[1]
    for r in range(dilation):
        blk = ref[r].astype(_F32)
        for s in range(n_slab):
            slab_ref[base + s, pl.ds(r, rows, stride=dilation), :] = blk[:, s * LANES:(s + 1) * LANES]
    return [slab_ref[base + s] for s in range(n_slab)]


def _mix_ffn_kernel(h_ref, oa0_ref, la0_ref, oa1_ref, la1_ref, oa2_ref, la2_ref, yb_ref,
                    gmix_ref, wgate_ref, bgate_ref, wpa_ref, wpb_ref, wout_ref,
                    g2_ref, wgu_ref, wd_ref, gfin_ref, y_ref, acc_ref, slab_ref):
    n_slab = A_OUT // LANES
    outs, lses = [], []
    for g, (o_ref, l_ref) in enumerate(((oa0_ref, la0_ref), (oa1_ref, la1_ref), (oa2_ref, la2_ref))):
        d = A_PAIRS[g][1]
        outs.append(_natural_order(o_ref, d, slab_ref, 2 * g * n_slab))
        lses.append(_natural_order(l_ref, d, slab_ref, (2 * g + 1) * n_slab))
    ya = []
    for s in range(n_slab):
        la0, la1, la2 = lses[0][s], lses[1][s], lses[2][s]
        mx = jnp.maximum(jnp.maximum(la0, la1), la2)
        e0, e1, e2 = jnp.exp(la0 - mx), jnp.exp(la1 - mx), jnp.exp(la2 - mx)
        ya.append((e0 * outs[0][s] + e1 * outs[1][s] + e2 * outs[2][s]) / (e0 + e1 + e2))
    ya = jnp.concatenate(ya, axis=1)
    h = h_ref[...]
    u = _rms(h, gmix_ref[...]).astype(_BF16)
    gates = _sigmoid(jnp.dot(u, wgate_ref[...], preferred_element_type=_F32) + bgate_ref[...])
    pa = jnp.dot(ya.astype(_BF16), wpa_ref[...], preferred_element_type=_F32)
    pb = jnp.dot(yb_ref[...], wpb_ref[...], preferred_element_type=_F32)
    mixed = gates[:, :D_MODEL] * pa + gates[:, D_MODEL:] * pb
    h = h + jnp.dot(mixed.astype(_BF16), wout_ref[...], preferred_element_type=_F32)
    hn = _rms(h, g2_ref[...]).astype(_BF16)
    _swiglu_into(acc_ref, hn, wgu_ref, wd_ref)
    h = h + 0.5 * acc_ref[...]
    y_ref[...] = _rms(h, gfin_ref[...])


def _resident(shape):
    nd = len(shape)
    return pl.BlockSpec(shape, lambda *_: (0,) * nd, pipeline_mode=pl.Buffered(1))


def _residue_spec(tm, d, width):
    return pl.BlockSpec((None, d, tm // d, width), lambda b, i: (b, 0, i, 0))


def _ffn_proj(x, p):
    batch, seq, _ = x.shape
    tm = TOKEN_TILE
    row = lambda w: pl.BlockSpec((None, tm, w), lambda b, i: (b, i, 0))
    consts = (p["g1"], p["wgu1"], p["wd1"], p["gmix"], p["win"], p["same_head"], p["gain"])
    dils = [d for _, d in A_PAIRS]
    return pl.pallas_call(
        _ffn_proj_kernel,
        grid=(batch, seq // tm),
        in_specs=[row(D_MODEL)] + [_resident(c.shape) for c in consts],
        out_specs=[row(D_MODEL)] + [_residue_spec(tm, d, GROUP_W) for d in dils] + [row(QKVB_W)],
        out_shape=[jax.ShapeDtypeStruct((batch, seq, D_MODEL), _F32)]
        + [jax.ShapeDtypeStruct((batch, d, seq // d, GROUP_W), _BF16) for d in dils]
        + [jax.ShapeDtypeStruct((batch, seq, QKVB_W), _BF16)],
        scratch_shapes=[pltpu.VMEM((tm, D_MODEL), _F32),
                        pltpu.VMEM((2 * GROUP_W // LANES, tm, LANES), _F32)],
        compiler_params=pltpu.CompilerParams(
            dimension_semantics=("arbitrary",) * 2, vmem_limit_bytes=V7X_VMEM_LIMIT_BYTES),
        name="ffn_proj",
    )(x, *consts)


def _attn_a(qkv, group):
    batch, d, length, _ = qkv.shape
    rows = min(length, ATTN_STEP_ROWS)
    planes = ATTN_STEP_ROWS // rows
    n_steps = length // rows
    halo = rows // A_HALF
    slopes = [float(2.0 ** (-8.0 * (group * A_SLOTS + h + 1) / A_HEADS)) for h in range(A_SLOTS)]

    def own(col):
        return pl.BlockSpec((None, planes, rows, A_OUT), lambda b, r, n: (b, r, n, col))

    def prev(col):
        return pl.BlockSpec((None, planes, A_HALF, A_OUT),
                            lambda b, r, n: (b, r, jnp.maximum(n * halo - 1, 0), col))

    def nxt(col):
        return pl.BlockSpec((None, planes, A_HALF, A_OUT),
                            lambda b, r, n: (b, r, jnp.minimum((n + 1) * halo, n_steps * halo - 1), col))

    return pl.pallas_call(
        functools.partial(_attn_a_kernel, dilation=d, slopes=slopes, n_steps=n_steps),
        grid=(batch, d // planes, n_steps),
        in_specs=[own(0), prev(1), own(1), nxt(1), prev(2), own(2), nxt(2)],
        out_specs=[own(0), own(0)],
        out_shape=[jax.ShapeDtypeStruct((batch, d, length, A_OUT), _BF16),
                   jax.ShapeDtypeStruct((batch, d, length, A_OUT), _F32)],
        scratch_shapes=[pltpu.VMEM((3, A_SLOTS * ATTN_BQ, ATTN_BQ + 2 * A_HALF), _F32)],
        compiler_params=pltpu.CompilerParams(dimension_semantics=("arbitrary",) * 3),
        name=f"attn_a{group}",
    )(qkv, qkv, qkv, qkv, qkv, qkv, qkv)


def _attn_b(qkvb, sink):
    batch, seq, _ = qkvb.shape
    rows = ATTN_STEP_ROWS
    n_steps = seq // rows
    halo = rows // B_HALF_WINDOW
    slopes = [float(2.0 ** (-8.0 * (h + 1) / B_Q_HEADS)) for h in range(B_Q_HEADS)]
    k_col = B_OUT // B_KV

    def own(col):
        return pl.BlockSpec((None, rows, B_KV), lambda b, n: (b, n, col))

    def prev(col):
        return pl.BlockSpec((None, B_HALF_WINDOW, B_KV),
                            lambda b, n: (b, jnp.maximum(n * halo - 1, 0), col))

    def nxt(col):
        return pl.BlockSpec((None, B_HALF_WINDOW, B_KV),
                            lambda b, n: (b, jnp.minimum((n + 1) * halo, n_steps * halo - 1), col))

    q_spec = pl.BlockSpec((None, rows, B_OUT), lambda b, n: (b, n, 0))
    return pl.pallas_call(
        functools.partial(_attn_b_kernel, slopes=slopes, n_steps=n_steps),
        grid=(batch, n_steps),
        in_specs=[q_spec, prev(k_col), own(k_col), nxt(k_col),
                  prev(k_col + 1), own(k_col + 1), nxt(k_col + 1),
                  pl.BlockSpec(memory_space=pltpu.SMEM)],
        out_specs=q_spec,
        out_shape=jax.ShapeDtypeStruct((batch, seq, B_OUT), _BF16),
        scratch_shapes=[pltpu.VMEM((3, B_GROUP * ATTN_BQ, B_KV_HEADS * (ATTN_BQ + 2 * B_HALF_WINDOW)), _F32)],
        compiler_params=pltpu.CompilerParams(dimension_semantics=("arbitrary",) * 2),
        name="attn_b",
    )(qkvb, qkvb, qkvb, qkvb, qkvb, qkvb, qkvb, sink)


def _mix_ffn(h, oa, la, yb, p):
    batch, seq, _ = h.shape
    tm = TOKEN_TILE
    row = lambda w: pl.BlockSpec((None, tm, w), lambda b, i: (b, i, 0))
    consts = (p["gmix"], p["wgate"], p["bgate"], p["wpa"], p["wpb"], p["wout"],
              p["g2"], p["wgu2"], p["wd2"], p["gfin"])
    attn_specs, attn_args = [], []
    for g, (_, d) in enumerate(A_PAIRS):
        attn_specs += [_residue_spec(tm, d, A_OUT)] * 2
        attn_args += [oa[g], la[g]]
    return pl.pallas_call(
        _mix_ffn_kernel,
        grid=(batch, seq // tm),
        in_specs=[row(D_MODEL)] + attn_specs + [row(B_OUT)] + [_resident(c.shape) for c in consts],
        out_specs=row(D_MODEL),
        out_shape=jax.ShapeDtypeStruct((batch, seq, D_MODEL), _F32),
        scratch_shapes=[pltpu.VMEM((tm, D_MODEL), _F32),
                        pltpu.VMEM((2 * N_GROUPS * A_OUT // LANES, tm, LANES), _F32)],
        compiler_params=pltpu.CompilerParams(
            dimension_semantics=("arbitrary",) * 2, vmem_limit_bytes=V7X_VMEM_LIMIT_BYTES),
        name="mix_ffn",
    )(h, *attn_args, yb, *consts)


def _b_head_perm():
    idx = [(kv * B_GROUP + c) * HEAD_DIM + e
           for c in range(B_GROUP) for kv in range(B_KV_HEADS) for e in range(HEAD_DIM)]
    return np.asarray(idx, dtype=np.int32)


def _prepare_params(ffn1_norm, ffn1_w_gate, ffn1_w_up, ffn1_w_down, mix_norm, w_in, a_q_norm,
                    a_k_norm, b_q_norm, b_k_norm, b_sink, w_proj_a, w_proj_b, w_gate, b_gate, w_out,
                    ffn2_norm, ffn2_w_gate, ffn2_w_up, ffn2_w_down, final_norm):
    def vec(g):
        return g.reshape(1, -1).astype(_F32)

    def bf16(w):
        return w.astype(_BF16)

    def gate_up(wg, wu):
        bounds = np.cumsum((0,) + FF_CHUNKS)
        return jnp.concatenate([w[:, lo:hi] for lo, hi in zip(bounds[:-1], bounds[1:]) for w in (wg, wu)],
                               axis=1).astype(_BF16)

    perm = _b_head_perm()
    qa, ka, va, qb, kb, vb = jnp.split(w_in, np.cumsum([A_W, A_W, A_W, B_OUT, B_KV]).tolist(), axis=1)
    grp = lambda w, g: w[:, g * A_OUT:(g + 1) * A_OUT]
    chunks = [m for g in range(N_GROUPS) for m in (grp(qa, g), grp(ka, g), grp(va, g))] + [qb[:, perm], kb, vb]
    win = jnp.concatenate(chunks, axis=1).astype(_BF16)
    scale = HEAD_DIM ** -0.5 * LOG2_E
    ones = lambda n: jnp.ones((n,), _F32)
    gain = jnp.concatenate(
        [jnp.tile(a_q_norm * scale, A_SLOTS), jnp.tile(a_k_norm, A_SLOTS), ones(A_OUT)] * N_GROUPS
        + [jnp.tile(b_q_norm * scale, B_Q_HEADS), jnp.tile(b_k_norm, B_KV_HEADS), ones(B_KV)])
    head_of_col = np.arange(MXU_TILE) // HEAD_DIM
    same_head = jnp.asarray(head_of_col[:, None] == head_of_col[None, :], dtype=_BF16)
    return dict(
        g1=vec(ffn1_norm), wgu1=gate_up(ffn1_w_gate, ffn1_w_up), wd1=bf16(ffn1_w_down),
        gmix=vec(mix_norm), win=win, same_head=same_head, gain=vec(gain), sink=b_sink.astype(_F32) * LOG2_E,
        wgate=w_gate.astype(_BF16), bgate=vec(b_gate), wpa=w_proj_a.astype(_BF16),
        wpb=w_proj_b[perm, :].astype(_BF16), wout=w_out.astype(_BF16),
        g2=vec(ffn2_norm), wgu2=gate_up(ffn2_w_gate, ffn2_w_up), wd2=bf16(ffn2_w_down),
        gfin=vec(final_norm),
    )


def _encoder(x, p):
    h, a0, a1, a2, qkvb = _ffn_proj(x, p)
    oa, la = zip(*[_attn_a(a, g) for g, a in enumerate((a0, a1, a2))])
    yb = _attn_b(qkvb, p["sink"])
    return _mix_ffn(h, oa, la, yb, p)


def kernel(x_prompt, x_sample, ffn1_norm, ffn1_w_gate, ffn1_w_up, ffn1_w_down, mix_norm, w_in, a_q_norm, a_k_norm, b_q_norm, b_k_norm, b_sink, w_proj_a, w_proj_b, w_gate, b_gate, w_out, ffn2_norm, ffn2_w_gate, ffn2_w_up, ffn2_w_down, final_norm):
    p = _prepare_params(*[w[0] for w in (
        ffn1_norm, ffn1_w_gate, ffn1_w_up, ffn1_w_down, mix_norm, w_in, a_q_norm, a_k_norm, b_q_norm,
        b_k_norm, b_sink, w_proj_a, w_proj_b, w_gate, b_gate, w_out, ffn2_norm, ffn2_w_gate, ffn2_w_up,
        ffn2_w_down, final_norm)])
    return (_encoder(x_prompt, p), _encoder(x_sample, p))
```

```python
import functools

import numpy as np
import jax
import jax.numpy as jnp
from jax import lax
from jax.experimental import pallas as pl
from jax.experimental.pallas import tpu as pltpu

D_MODEL = 1024
D_FF = 2816
HEAD_DIM = 64
A_PAIRS = ((128, 1), (512, 4), (2048, 16))
N_GROUPS = len(A_PAIRS)
A_SLOTS = 4
A_HEADS = A_SLOTS * N_GROUPS
A_OUT = A_SLOTS * HEAD_DIM
A_W = A_HEADS * HEAD_DIM
B_Q_HEADS = 8
B_KV_HEADS = 2
B_GROUP = B_Q_HEADS // B_KV_HEADS
B_HALF_WINDOW = 128
B_OUT = B_Q_HEADS * HEAD_DIM
B_KV = B_KV_HEADS * HEAD_DIM
NORM_EPS = 1e-6
MASK_VALUE = -1e30
LOG2_E = 1.4426950408889634
LN_2 = 0.6931471805599453

MXU_TILE = 256
GROUP_W = 3 * A_OUT
QKVB_W = B_OUT + 2 * B_KV

LANES = 128
FF_CHUNKS = (512,) * 5 + (256,)
TOKEN_TILE = 512
ATTN_BQ = 128
ATTN_STEP_ROWS = 1024
A_HALF = 64
V7X_VMEM_LIMIT_BYTES = 56 * 1024 * 1024

_F32 = jnp.float32
_BF16 = jnp.bfloat16


def _rms(x, g):
    ms = jnp.mean(x * x, axis=-1, keepdims=True)
    return x * lax.rsqrt(ms + NORM_EPS) * g


def _swiglu_into(acc_ref, xn, wg_half_ref, wu_ref, wd_ref):
    lo = 0
    for w in FF_CHUNKS:
        g = jnp.dot(xn, wg_half_ref[:, lo:lo + w], preferred_element_type=_F32)
        u = jnp.dot(xn, wu_ref[:, lo:lo + w], preferred_element_type=_F32)
        a = (g * (1.0 + jnp.tanh(g)) * u).astype(_BF16)
        part = jnp.dot(a, wd_ref[lo:lo + w, :], preferred_element_type=_F32)
        if lo == 0:
            acc_ref[...] = part
        else:
            acc_ref[...] += part
        lo += w


def _qk_normed(chunk, n_norm, same_head_ref, gain):
    parts = []
    for lo in range(0, n_norm, MXU_TILE):
        w = min(MXU_TILE, n_norm - lo)
        x = chunk[:, lo:lo + w]
        ssq = jnp.dot((x * x).astype(_BF16), same_head_ref[:w, :w], preferred_element_type=_F32)
        parts.append(x * lax.rsqrt(ssq * (1.0 / HEAD_DIM) + NORM_EPS))
    return jnp.concatenate(parts, axis=1) * gain, chunk[:, n_norm:]


def _ffn_proj_kernel(x_ref, g1_ref, wg_ref, wu_ref, wd_ref, gmix_ref, win_ref, same_head_ref,
                     gain_ref, h_ref, a0_ref, a1_ref, a2_ref, qkvb_ref, acc_ref, slab_ref):
    tm = x_ref.shape[0]
    x = x_ref[...]
    xn = _rms(x, g1_ref[...]).astype(_BF16)
    _swiglu_into(acc_ref, xn, wg_ref, wu_ref, wd_ref)
    h = x + 0.5 * acc_ref[...]
    h_ref[...] = h
    u = _rms(h, gmix_ref[...]).astype(_BF16)
    n_slab = GROUP_W // LANES

    def project_group(g, a_ref):
        lo = g * GROUP_W
        chunk = jnp.dot(u, win_ref[:, lo:lo + GROUP_W], preferred_element_type=_F32)
        nq, val = _qk_normed(chunk, 2 * A_OUT, same_head_ref, gain_ref[:, lo:lo + 2 * A_OUT])
        d = A_PAIRS[g][1]
        if d == 1:
            a_ref[0, :, :2 * A_OUT] = nq.astype(_BF16)
            a_ref[0, :, 2 * A_OUT:] = val.astype(_BF16)
            return
        base = (g - 1) * n_slab
        for s in range(n_slab):
            src = nq if s * LANES < 2 * A_OUT else val
            col = s * LANES % (2 * A_OUT)
            slab_ref[base + s] = src[:, col:col + LANES]
        for r in range(d):
            for s in range(n_slab):
                a_ref[r, :, s * LANES:(s + 1) * LANES] = (
                    slab_ref[base + s, pl.ds(r, tm // d, stride=d), :].astype(_BF16))

    def project_b():
        lo = N_GROUPS * GROUP_W
        chunk = jnp.dot(u, win_ref[:, lo:lo + QKVB_W], preferred_element_type=_F32)
        nq, val = _qk_normed(chunk, B_OUT + B_KV, same_head_ref, gain_ref[:, lo:lo + B_OUT + B_KV])
        qkvb_ref[:, :B_OUT + B_KV] = nq.astype(_BF16)
        qkvb_ref[:, B_OUT + B_KV:] = val.astype(_BF16)

    project_group(2, a2_ref)
    project_group(1, a1_ref)
    project_group(0, a0_ref)
    project_b()


def _band_bias(rows_per_head, n_heads, bk, half, slopes, dist_scale, key_lo, key_hi):
    shape = (n_heads * rows_per_head, bk)
    row = lax.broadcasted_iota(jnp.int32, shape, 0)
    col = lax.broadcasted_iota(jnp.int32, shape, 1)
    rel = jnp.abs(col - half - (row % rows_per_head))
    head = row // rows_per_head
    slope = jnp.full(shape, slopes[0], _F32)
    for h in range(1, n_heads):
        slope = jnp.where(head == h, slopes[h], slope)
    dist = (rel * dist_scale).astype(_F32)
    ok = (rel <= half) & (col >= key_lo) & (col < key_hi)
    return jnp.where(ok, -slope * LOG2_E * dist, MASK_VALUE)


def _init_bias_tables(bias_ref, rows_per_head, n_heads, half, slopes, dist_scale):
    bk = rows_per_head + 2 * half
    for t, (lo, hi) in enumerate(((0, bk), (half, bk), (0, rows_per_head + half))):
        bias_ref[t] = _band_bias(rows_per_head, n_heads, bk, half, slopes, dist_scale, lo, hi)


def _table_index(j, n_sub, step, n_steps):
    if j == 0:
        return jnp.where(step == 0, 1, 0)
    if j == n_sub - 1:
        return jnp.where(step == n_steps - 1, 2, 0)
    return 0


def _attn_a_kernel(q_ref, kp_ref, ko_ref, kn_ref, vp_ref, vo_ref, vn_ref, o_ref, lse_ref, bias_ref,
                   *, dilation, slopes, n_steps):
    step = pl.program_id(2)
    bq = ATTN_BQ
    bk = bq + 2 * A_HALF
    planes, rows, _ = q_ref.shape
    n_sub = rows // bq

    @pl.when((pl.program_id(0) == 0) & (pl.program_id(1) == 0) & (step == 0))
    def _():
        _init_bias_tables(bias_ref, bq, A_SLOTS, A_HALF, slopes, dilation)

    first_of_pair = lax.broadcasted_iota(jnp.int32, (bq, LANES), 1) < HEAD_DIM
    for r in range(planes):
        k_all = jnp.concatenate([kp_ref[r], ko_ref[r], kn_ref[r]], axis=0)
        v_all = jnp.concatenate([vp_ref[r], vo_ref[r], vn_ref[r]], axis=0)
        for j in range(n_sub):
            zero = jnp.zeros((bq, LANES), _BF16)
            blocks = []
            for h in range(A_SLOTS):
                q_half = q_ref[r, j * bq:(j + 1) * bq, (h // 2) * LANES:(h // 2 + 1) * LANES]
                own = jnp.where(first_of_pair == (h % 2 == 0), q_half, zero)
                blocks.append(jnp.concatenate([own, zero] if h < 2 else [zero, own], axis=1))
            qs = jnp.concatenate(blocks, axis=0)
            k = k_all[j * bq:j * bq + bk]
            v = v_all[j * bq:j * bq + bk]
            s = lax.dot_general(qs, k, (((1,), (1,)), ((), ())), preferred_element_type=_F32)
            s = s + bias_ref[_table_index(j, n_sub, step, n_steps)]
            m = jnp.max(s, axis=1, keepdims=True)
            p = jnp.exp2(s - m)
            l = jnp.sum(p, axis=1, keepdims=True)
            pv = jnp.dot(p.astype(_BF16), v, preferred_element_type=_F32)
            inv_l = 1.0 / l
            lse2 = m + jnp.log2(l)
            o_halves, lse_halves = [], []
            for pair in range(A_SLOTS // 2):
                lanes = slice(pair * LANES, (pair + 1) * LANES)
                r0 = slice(2 * pair * bq, (2 * pair + 1) * bq)
                r1 = slice((2 * pair + 1) * bq, (2 * pair + 2) * bq)
                o_halves.append(jnp.where(first_of_pair, pv[r0, lanes] * inv_l[r0], pv[r1, lanes] * inv_l[r1]))
                lse_halves.append(jnp.where(first_of_pair, jnp.broadcast_to(lse2[r0], (bq, LANES)),
                                            jnp.broadcast_to(lse2[r1], (bq, LANES))))
            o_ref[r, j * bq:(j + 1) * bq, :] = jnp.concatenate(o_halves, axis=1).astype(o_ref.dtype)
            lse_ref[r, j * bq:(j + 1) * bq, :] = jnp.concatenate(lse_halves, axis=1)


def _init_bias_tables_b(bias_ref, slopes):
    bq, half = ATTN_BQ, B_HALF_WINDOW
    bk = bq + 2 * half
    shape = (B_GROUP * bq, B_KV_HEADS * bk)
    row = lax.broadcasted_iota(jnp.int32, shape, 0)
    col = lax.broadcasted_iota(jnp.int32, shape, 1)
    key = col % bk
    head = (col // bk) * B_GROUP + row // bq
    rel = jnp.abs(key - half - (row % bq))
    slope = jnp.full(shape, slopes[0], _F32)
    for h in range(1, B_Q_HEADS):
        slope = jnp.where(head == h, slopes[h], slope)
    bias = -slope * LOG2_E * rel.astype(_F32)
    for t, (lo, hi) in enumerate(((0, bk), (half, bk), (0, bq + half))):
        bias_ref[t] = jnp.where((rel <= half) & (key >= lo) & (key < hi), bias, MASK_VALUE)


def _attn_b_kernel(q_ref, kp_ref, ko_ref, kn_ref, vp_ref, vo_ref, vn_ref, sink_ref, o_ref, bias_ref,
                   *, slopes, n_steps):
    step = pl.program_id(1)
    bq = ATTN_BQ
    bk = bq + 2 * B_HALF_WINDOW
    n_sub = q_ref.shape[0] // bq

    @pl.when((pl.program_id(0) == 0) & (step == 0))
    def _():
        _init_bias_tables_b(bias_ref, slopes)

    k_all = jnp.concatenate([kp_ref[...], ko_ref[...], kn_ref[...]], axis=0)
    v_all = jnp.concatenate([vp_ref[...], vo_ref[...], vn_ref[...]], axis=0)
    lane_kv = lax.broadcasted_iota(jnp.int32, (bq, B_KV), 1) // HEAD_DIM
    key_lane_kv = lax.broadcasted_iota(jnp.int32, k_all.shape, 1) // HEAD_DIM
    zero = jnp.zeros_like(k_all)
    k_kv = [jnp.where(key_lane_kv == kv, k_all, zero) for kv in range(B_KV_HEADS)]
    v_kv = [jnp.concatenate([jnp.where(key_lane_kv == kv, v_all, zero),
                             jnp.where(key_lane_kv == kv, 1.0, 0.0).astype(_BF16)], axis=1)
            for kv in range(B_KV_HEADS)]
    for j in range(n_sub):
        keys = slice(j * bq, j * bq + bk)
        qs = jnp.concatenate([q_ref[j * bq:(j + 1) * bq, c * B_KV:(c + 1) * B_KV]
                              for c in range(B_GROUP)], axis=0)
        k = jnp.concatenate([k_kv[kv][keys] for kv in range(B_KV_HEADS)], axis=0)
        v = jnp.concatenate([v_kv[kv][keys] for kv in range(B_KV_HEADS)], axis=0)
        s = lax.dot_general(qs, k, (((1,), (1,)), ((), ())), preferred_element_type=_F32)
        s = s + bias_ref[_table_index(j, n_sub, step, n_steps)]
        ps, sink_terms = [], []
        for c in range(B_GROUP):
            p_c, e_c = [], []
            for kv in range(B_KV_HEADS):
                sink = sink_ref[kv * B_GROUP + c]
                s_h = s[c * bq:(c + 1) * bq, kv * bk:(kv + 1) * bk]
                m_h = jnp.maximum(jnp.max(s_h, axis=1, keepdims=True), sink)
                p_c.append(jnp.exp2(s_h - m_h).astype(_BF16))
                e_c.append(jnp.exp2(sink - m_h))
            ps.append(jnp.concatenate(p_c, axis=1))
            sink_terms.append(e_c)
        pv = jnp.dot(jnp.concatenate(ps, axis=0), v, preferred_element_type=_F32)
        for c in range(B_GROUP):
            rows = pv[c * bq:(c + 1) * bq]
            den = rows[:, B_KV:] + jnp.where(lane_kv == 0, sink_terms[c][0], sink_terms[c][1])
            o_ref[j * bq:(j + 1) * bq, c * B_KV:(c + 1) * B_KV] = (rows[:, :B_KV] / den).astype(o_ref.dtype)


def _natural_order(ref, dilation, slab_ref, base):
    n_slab = A_OUT // LANES
    if dilation == 1:
        x = ref[0].astype(_F32)
        return [x[:, s * LANES:(s + 1) * LANES] for s in range(n_slab)]
    rows = ref.shape[1]
    for r in range(dilation):
        blk = ref[r].astype(_F32)
        for s in range(n_slab):
            slab_ref[base + s, pl.ds(r, rows, stride=dilation), :] = blk[:, s * LANES:(s + 1) * LANES]
    return [slab_ref[base + s] for s in range(n_slab)]


def _mix_ffn_kernel(h_ref, oa0_ref, la0_ref, oa1_ref, la1_ref, oa2_ref, la2_ref, yb_ref,
                    gmix_ref, wgate_ref, bgate_ref, wpa_ref, wpb_ref, wout_ref,
                    g2_ref, wg_ref, wu_ref, wd_ref, gfin_ref, y_ref, acc_ref, slab_ref):
    n_slab = A_OUT // LANES
    outs, lses = [], []
    for g, (o_ref, l_ref) in enumerate(((oa0_ref, la0_ref), (oa1_ref, la1_ref), (oa2_ref, la2_ref))):
        d = A_PAIRS[g][1]
        outs.append(_natural_order(o_ref, d, slab_ref, 2 * g * n_slab))
        lses.append(_natural_order(l_ref, d, slab_ref, (2 * g + 1) * n_slab))
    ya = []
    for s in range(n_slab):
        la0, la1, la2 = lses[0][s], lses[1][s], lses[2][s]
        mx = jnp.maximum(jnp.maximum(la0, la1), la2)
        e0, e1, e2 = jnp.exp2(la0 - mx), jnp.exp2(la1 - mx), jnp.exp2(la2 - mx)
        ya.append((e0 * outs[0][s] + e1 * outs[1][s] + e2 * outs[2][s]) / (e0 + e1 + e2))
    ya = jnp.concatenate(ya, axis=1)
    h = h_ref[...]
    u = _rms(h, gmix_ref[...]).astype(_BF16)
    t = jnp.tanh(jnp.dot(u, wgate_ref[...], preferred_element_type=_F32) + bgate_ref[...])
    pa = jnp.dot(ya.astype(_BF16), wpa_ref[...], preferred_element_type=_F32)
    pb = jnp.dot(yb_ref[...], wpb_ref[...], preferred_element_type=_F32)
    mixed = (pa + pb) + t[:, :D_MODEL] * pa + t[:, D_MODEL:] * pb
    h = h + jnp.dot(mixed.astype(_BF16), wout_ref[...], preferred_element_type=_F32)
    hn = _rms(h, g2_ref[...]).astype(_BF16)
    _swiglu_into(acc_ref, hn, wg_ref, wu_ref, wd_ref)
    h = h + 0.5 * acc_ref[...]
    y_ref[...] = _rms(h, gfin_ref[...])


def _resident(shape):
    nd = len(shape)
    return pl.BlockSpec(shape, lambda *_: (0,) * nd, pipeline_mode=pl.Buffered(1))


def _residue_spec(tm, d, width):
    return pl.BlockSpec((None, d, tm // d, width), lambda b, i: (b, 0, i, 0))


def _ffn_proj(x, p):
    batch, seq, _ = x.shape
    tm = TOKEN_TILE
    row = lambda w: pl.BlockSpec((None, tm, w), lambda b, i: (b, i, 0))
    consts = (p["g1"], p["wg1"], p["wu1"], p["wd1"], p["gmix"], p["win"], p["same_head"], p["gain"])
    dils = [d for _, d in A_PAIRS]
    return pl.pallas_call(
        _ffn_proj_kernel,
        grid=(batch, seq // tm),
        in_specs=[row(D_MODEL)] + [_resident(c.shape) for c in consts],
        out_specs=[row(D_MODEL)] + [_residue_spec(tm, d, GROUP_W) for d in dils] + [row(QKVB_W)],
        out_shape=[jax.ShapeDtypeStruct((batch, seq, D_MODEL), _F32)]
        + [jax.ShapeDtypeStruct((batch, d, seq // d, GROUP_W), _BF16) for d in dils]
        + [jax.ShapeDtypeStruct((batch, seq, QKVB_W), _BF16)],
        scratch_shapes=[pltpu.VMEM((tm, D_MODEL), _F32),
                        pltpu.VMEM((2 * GROUP_W // LANES, tm, LANES), _F32)],
        compiler_params=pltpu.CompilerParams(
            dimension_semantics=("arbitrary",) * 2, vmem_limit_bytes=V7X_VMEM_LIMIT_BYTES),
        name="ffn_proj",
    )(x, *consts)


def _attn_a(qkv, group):
    batch, d, length, _ = qkv.shape
    rows = min(length, ATTN_STEP_ROWS)
    planes = ATTN_STEP_ROWS // rows
    n_steps = length // rows
    halo = rows // A_HALF
    slopes = [float(2.0 ** (-8.0 * (group * A_SLOTS + h + 1) / A_HEADS)) for h in range(A_SLOTS)]

    def own(col):
        return pl.BlockSpec((None, planes, rows, A_OUT), lambda b, r, n: (b, r, n, col))

    def prev(col):
        return pl.BlockSpec((None, planes, A_HALF, A_OUT),
                            lambda b, r, n: (b, r, jnp.maximum(n * halo - 1, 0), col))

    def nxt(col):
        return pl.BlockSpec((None, planes, A_HALF, A_OUT),
                            lambda b, r, n: (b, r, jnp.minimum((n + 1) * halo, n_steps * halo - 1), col))

    return pl.pallas_call(
        functools.partial(_attn_a_kernel, dilation=d, slopes=slopes, n_steps=n_steps),
        grid=(batch, d // planes, n_steps),
        in_specs=[own(0), prev(1), own(1), nxt(1), prev(2), own(2), nxt(2)],
        out_specs=[own(0), own(0)],
        out_shape=[jax.ShapeDtypeStruct((batch, d, length, A_OUT), _BF16),
                   jax.ShapeDtypeStruct((batch, d, length, A_OUT), _F32)],
        scratch_shapes=[pltpu.VMEM((3, A_SLOTS * ATTN_BQ, ATTN_BQ + 2 * A_HALF), _F32)],
        compiler_params=pltpu.CompilerParams(dimension_semantics=("arbitrary",) * 3),
        name=f"attn_a{group}",
    )(qkv, qkv, qkv, qkv, qkv, qkv, qkv)


def _attn_b(qkvb, sink):
    batch, seq, _ = qkvb.shape
    rows = ATTN_STEP_ROWS
    n_steps = seq // rows
    halo = rows // B_HALF_WINDOW
    slopes = [float(2.0 ** (-8.0 * (h + 1) / B_Q_HEADS)) for h in range(B_Q_HEADS)]
    k_col = B_OUT // B_KV

    def own(col):
        return pl.BlockSpec((None, rows, B_KV), lambda b, n: (b, n, col))

    def prev(col):
        return pl.BlockSpec((None, B_HALF_WINDOW, B_KV),
                            lambda b, n: (b, jnp.maximum(n * halo - 1, 0), col))

    def nxt(col):
        return pl.BlockSpec((None, B_HALF_WINDOW, B_KV),
                            lambda b, n: (b, jnp.minimum((n + 1) * halo, n_steps * halo - 1), col))

    q_spec = pl.BlockSpec((None, rows, B_OUT), lambda b, n: (b, n, 0))
    return pl.pallas_call(
        functools.partial(_attn_b_kernel, slopes=slopes, n_steps=n_steps),
        grid=(batch, n_steps),
        in_specs=[q_spec, prev(k_col), own(k_col), nxt(k_col),
                  prev(k_col + 1), own(k_col + 1), nxt(k_col + 1),
                  pl.BlockSpec(memory_space=pltpu.SMEM)],
        out_specs=q_spec,
        out_shape=jax.ShapeDtypeStruct((batch, seq, B_OUT), _BF16),
        scratch_shapes=[pltpu.VMEM((3, B_GROUP * ATTN_BQ, B_KV_HEADS * (ATTN_BQ + 2 * B_HALF_WINDOW)), _F32)],
        compiler_params=pltpu.CompilerParams(dimension_semantics=("arbitrary",) * 2),
        name="attn_b",
    )(qkvb, qkvb, qkvb, qkvb, qkvb, qkvb, qkvb, sink)


def _mix_ffn(h, oa, la, yb, p):
    batch, seq, _ = h.shape
    tm = TOKEN_TILE
    row = lambda w: pl.BlockSpec((None, tm, w), lambda b, i: (b, i, 0))
    consts = (p["gmix"], p["wgate"], p["bgate"], p["wpa"], p["wpb"], p["wout"],
              p["g2"], p["wg2"], p["wu2"], p["wd2"], p["gfin"])
    attn_specs, attn_args = [], []
    for g, (_, d) in enumerate(A_PAIRS):
        attn_specs += [_residue_spec(tm, d, A_OUT)] * 2
        attn_args += [oa[g], la[g]]
    return pl.pallas_call(
        _mix_ffn_kernel,
        grid=(batch, seq // tm),
        in_specs=[row(D_MODEL)] + attn_specs + [row(B_OUT)] + [_resident(c.shape) for c in consts],
        out_specs=row(D_MODEL),
        out_shape=jax.ShapeDtypeStruct((batch, seq, D_MODEL), _F32),
        scratch_shapes=[pltpu.VMEM((tm, D_MODEL), _F32),
                        pltpu.VMEM((2 * N_GROUPS * A_OUT // LANES, tm, LANES), _F32)],
        compiler_params=pltpu.CompilerParams(
            dimension_semantics=("arbitrary",) * 2, vmem_limit_bytes=V7X_VMEM_LIMIT_BYTES),
        name="mix_ffn",
    )(h, *attn_args, yb, *consts)


def _b_head_perm():
    idx = [(kv * B_GROUP + c) * HEAD_DIM + e
           for c in range(B_GROUP) for kv in range(B_KV_HEADS) for e in range(HEAD_DIM)]
    return np.asarray(idx, dtype=np.int32)


def _prepare_params(ffn1_norm, ffn1_w_gate, ffn1_w_up, ffn1_w_down, mix_norm, w_in, a_q_norm,
                    a_k_norm, b_q_norm, b_k_norm, b_sink, w_proj_a, w_proj_b, w_gate, b_gate, w_out,
                    ffn2_norm, ffn2_w_gate, ffn2_w_up, ffn2_w_down, final_norm):
    def vec(g):
        return g.reshape(1, -1).astype(_F32)

    def bf16(w):
        return w.astype(_BF16)

    def half(w):
        return (0.5 * w).astype(_BF16)

    perm = _b_head_perm()
    qa, ka, va, qb, kb, vb = jnp.split(w_in, np.cumsum([A_W, A_W, A_W, B_OUT, B_KV]).tolist(), axis=1)
    grp = lambda w, g: w[:, g * A_OUT:(g + 1) * A_OUT]
    chunks = [m for g in range(N_GROUPS) for m in (grp(qa, g), grp(ka, g), grp(va, g))] + [qb[:, perm], kb, vb]
    win = jnp.concatenate(chunks, axis=1).astype(_BF16)
    scale = HEAD_DIM ** -0.5 * LOG2_E
    ones = lambda n: jnp.ones((n,), _F32)
    gain = jnp.concatenate(
        [jnp.tile(a_q_norm * scale, A_SLOTS), jnp.tile(a_k_norm, A_SLOTS), ones(A_OUT)] * N_GROUPS
        + [jnp.tile(b_q_norm * scale, B_Q_HEADS), jnp.tile(b_k_norm, B_KV_HEADS), ones(B_KV)])
    head_of_col = np.arange(MXU_TILE) // HEAD_DIM
    same_head = jnp.asarray(head_of_col[:, None] == head_of_col[None, :], dtype=_BF16)
    return dict(
        g1=vec(ffn1_norm), wg1=half(ffn1_w_gate), wu1=bf16(ffn1_w_up), wd1=bf16(ffn1_w_down),
        gmix=vec(mix_norm), win=win, same_head=same_head, gain=vec(gain), sink=b_sink.astype(_F32) * LOG2_E,
        wgate=half(w_gate), bgate=vec(0.5 * b_gate), wpa=w_proj_a.astype(_BF16),
        wpb=w_proj_b[perm, :].astype(_BF16), wout=half(w_out),
        g2=vec(ffn2_norm), wg2=half(ffn2_w_gate), wu2=bf16(ffn2_w_up), wd2=bf16(ffn2_w_down),
        gfin=vec(final_norm),
    )


def _encoder(x, p):
    h, a0, a1, a2, qkvb = _ffn_proj(x, p)
    oa, la = zip(*[_attn_a(a, g) for g, a in enumerate((a0, a1, a2))])
    yb = _attn_b(qkvb, p["sink"])
    return _mix_ffn(h, oa, la, yb, p)


def kernel(x_prompt, x_sample, ffn1_norm, ffn1_w_gate, ffn1_w_up, ffn1_w_down, mix_norm, w_in, a_q_norm, a_k_norm, b_q_norm, b_k_norm, b_sink, w_proj_a, w_proj_b, w_gate, b_gate, w_out, ffn2_norm, ffn2_w_gate, ffn2_w_up, ffn2_w_down, final_norm):
    p = _prepare_params(*[w[0] for w in (
        ffn1_norm, ffn1_w_gate, ffn1_w_up, ffn1_w_down, mix_norm, w_in, a_q_norm, a_k_norm, b_q_norm,
        b_k_norm, b_sink, w_proj_a, w_proj_b, w_gate, b_gate, w_out, ffn2_norm, ffn2_w_gate, ffn2_w_up,
        ffn2_w_down, final_norm)])
    return (_encoder(x_prompt, p), _encoder(x_sample, p))
```

```python
import functools

import numpy as np
import jax
import jax.numpy as jnp
from jax import lax
from jax.experimental import pallas as pl
from jax.experimental.pallas import tpu as pltpu

D_MODEL = 1024
D_FF = 2816
HEAD_DIM = 64
A_PAIRS = ((128, 1), (512, 4), (2048, 16))
N_GROUPS = len(A_PAIRS)
A_SLOTS = 4
A_HEADS = A_SLOTS * N_GROUPS
A_OUT = A_SLOTS * HEAD_DIM
A_W = A_HEADS * HEAD_DIM
B_Q_HEADS = 8
B_KV_HEADS = 2
B_GROUP = B_Q_HEADS // B_KV_HEADS
B_HALF_WINDOW = 128
B_OUT = B_Q_HEADS * HEAD_DIM
B_KV = B_KV_HEADS * HEAD_DIM
NORM_EPS = 1e-6
MASK_VALUE = -1e30
LOG2_E = 1.4426950408889634

MXU_TILE = 256
GROUP_W = 3 * A_OUT
QKVB_W = B_OUT + 2 * B_KV

LANES = 128
FF_CHUNKS = (512,) * 5 + (256,)
TOKEN_TILE = 512
ATTN_BQ = 128
ATTN_STEP_ROWS = 2048
A_HALF = 64
V7X_VMEM_LIMIT_BYTES = 56 * 1024 * 1024

_F32 = jnp.float32
_BF16 = jnp.bfloat16


def _rms(x, g):
    ms = jnp.mean(x * x, axis=-1, keepdims=True)
    return x * lax.rsqrt(ms + NORM_EPS) * g


def _swiglu_into(acc_ref, xn, wg_half_ref, wu_ref, wd_ref):
    lo = 0
    for w in FF_CHUNKS:
        g = jnp.dot(xn, wg_half_ref[:, lo:lo + w], preferred_element_type=_F32)
        u = jnp.dot(xn, wu_ref[:, lo:lo + w], preferred_element_type=_F32)
        a = (g * (1.0 + jnp.tanh(g)) * u).astype(_BF16)
        part = jnp.dot(a, wd_ref[lo:lo + w, :], preferred_element_type=_F32)
        if lo == 0:
            acc_ref[...] = part
        else:
            acc_ref[...] += part
        lo += w


def _qk_normed(chunk, n_norm, same_head_ref, gain):
    parts = []
    for lo in range(0, n_norm, MXU_TILE):
        w = min(MXU_TILE, n_norm - lo)
        x = chunk[:, lo:lo + w]
        ssq = jnp.dot((x * x).astype(_BF16), same_head_ref[:w, :w], preferred_element_type=_F32)
        parts.append(x * lax.rsqrt(ssq * (1.0 / HEAD_DIM) + NORM_EPS))
    return jnp.concatenate(parts, axis=1) * gain, chunk[:, n_norm:]


def _ffn_proj_kernel(x_ref, g1_ref, wg_ref, wu_ref, wd_ref, gmix_ref, win_ref, same_head_ref,
                     gain_ref, h_ref, u_ref, a0_ref, a1_ref, a2_ref, qkvb_ref, acc_ref, slab_ref):
    tm = x_ref.shape[0]
    x = x_ref[...]
    xn = _rms(x, g1_ref[...]).astype(_BF16)
    _swiglu_into(acc_ref, xn, wg_ref, wu_ref, wd_ref)
    h = x + 0.5 * acc_ref[...]
    h_ref[...] = h
    u = _rms(h, gmix_ref[...]).astype(_BF16)
    u_ref[...] = u
    n_slab = GROUP_W // LANES

    def project_group(g, a_ref):
        lo = g * GROUP_W
        chunk = jnp.dot(u, win_ref[:, lo:lo + GROUP_W], preferred_element_type=_F32)
        nq, val = _qk_normed(chunk, 2 * A_OUT, same_head_ref, gain_ref[:, lo:lo + 2 * A_OUT])
        d = A_PAIRS[g][1]
        if d == 1:
            a_ref[0, :, :2 * A_OUT] = nq.astype(_BF16)
            a_ref[0, :, 2 * A_OUT:] = val.astype(_BF16)
            return
        base = (g - 1) * n_slab
        for s in range(n_slab):
            src = nq if s * LANES < 2 * A_OUT else val
            col = s * LANES % (2 * A_OUT)
            slab_ref[base + s] = src[:, col:col + LANES]
        for r in range(d):
            for s in range(n_slab):
                a_ref[r, :, s * LANES:(s + 1) * LANES] = (
                    slab_ref[base + s, pl.ds(r, tm // d, stride=d), :].astype(_BF16))

    def project_b():
        lo = N_GROUPS * GROUP_W
        chunk = jnp.dot(u, win_ref[:, lo:lo + QKVB_W], preferred_element_type=_F32)
        nq, val = _qk_normed(chunk, B_OUT + B_KV, same_head_ref, gain_ref[:, lo:lo + B_OUT + B_KV])
        qkvb_ref[:, :B_OUT + B_KV] = nq.astype(_BF16)
        qkvb_ref[:, B_OUT + B_KV:] = val.astype(_BF16)

    project_group(2, a2_ref)
    project_group(1, a1_ref)
    project_group(0, a0_ref)
    project_b()


def _band_bias(rows_per_head, n_heads, bk, half, slopes, dist_scale, key_lo, key_hi):
    shape = (n_heads * rows_per_head, bk)
    row = lax.broadcasted_iota(jnp.int32, shape, 0)
    col = lax.broadcasted_iota(jnp.int32, shape, 1)
    rel = jnp.abs(col - half - (row % rows_per_head))
    head = row // rows_per_head
    slope = jnp.full(shape, slopes[0], _F32)
    for h in range(1, n_heads):
        slope = jnp.where(head == h, slopes[h], slope)
    dist = (rel * dist_scale).astype(_F32)
    ok = (rel <= half) & (col >= key_lo) & (col < key_hi)
    return jnp.where(ok, -slope * LOG2_E * dist, MASK_VALUE)


def _init_bias_tables(bias_ref, rows_per_head, n_heads, half, slopes, dist_scale):
    bk = rows_per_head + 2 * half
    for t, (lo, hi) in enumerate(((0, bk), (half, bk), (0, rows_per_head + half))):
        bias_ref[t] = _band_bias(rows_per_head, n_heads, bk, half, slopes, dist_scale, lo, hi)


def _table_index(j, n_sub, step, n_steps):
    if j == 0:
        return jnp.where(step == 0, 1, 0)
    if j == n_sub - 1:
        return jnp.where(step == n_steps - 1, 2, 0)
    return 0


def _attn_a_kernel(q_ref, kp_ref, ko_ref, kn_ref, vp_ref, vo_ref, vn_ref, o_ref, lse_ref, bias_ref,
                   *, dilation, slopes, n_steps):
    step = pl.program_id(2)
    bq = ATTN_BQ
    bk = bq + 2 * A_HALF
    planes, rows, _ = q_ref.shape
    n_sub = rows // bq

    @pl.when((pl.program_id(0) == 0) & (pl.program_id(1) == 0) & (step == 0))
    def _():
        _init_bias_tables(bias_ref, bq, A_SLOTS, A_HALF, slopes, dilation)

    first_of_pair = lax.broadcasted_iota(jnp.int32, (bq, LANES), 1) < HEAD_DIM
    for r in range(planes):
        k_all = jnp.concatenate([kp_ref[r], ko_ref[r], kn_ref[r]], axis=0)
        v_all = jnp.concatenate([vp_ref[r], vo_ref[r], vn_ref[r]], axis=0)
        for j in range(n_sub):
            zero = jnp.zeros((bq, LANES), _BF16)
            blocks = []
            for h in range(A_SLOTS):
                q_half = q_ref[r, j * bq:(j + 1) * bq, (h // 2) * LANES:(h // 2 + 1) * LANES]
                own = jnp.where(first_of_pair == (h % 2 == 0), q_half, zero)
                blocks.append(jnp.concatenate([own, zero] if h < 2 else [zero, own], axis=1))
            qs = jnp.concatenate(blocks, axis=0)
            k = k_all[j * bq:j * bq + bk]
            v = v_all[j * bq:j * bq + bk]
            s = lax.dot_general(qs, k, (((1,), (1,)), ((), ())), preferred_element_type=_F32)
            s = s + bias_ref[_table_index(j, n_sub, step, n_steps)]
            m = jnp.max(s, axis=1, keepdims=True)
            p = jnp.exp2(s - m)
            l = jnp.sum(p, axis=1, keepdims=True)
            pv = jnp.dot(p.astype(_BF16), v, preferred_element_type=_F32)
            inv_l = 1.0 / l
            lse2 = m + jnp.log2(l)
            o_halves, lse_halves = [], []
            for pair in range(A_SLOTS // 2):
                lanes = slice(pair * LANES, (pair + 1) * LANES)
                r0 = slice(2 * pair * bq, (2 * pair + 1) * bq)
                r1 = slice((2 * pair + 1) * bq, (2 * pair + 2) * bq)
                o_halves.append(jnp.where(first_of_pair, pv[r0, lanes] * inv_l[r0], pv[r1, lanes] * inv_l[r1]))
                lse_halves.append(jnp.where(first_of_pair, jnp.broadcast_to(lse2[r0], (bq, LANES)),
                                            jnp.broadcast_to(lse2[r1], (bq, LANES))))
            o_ref[r, j * bq:(j + 1) * bq, :] = jnp.concatenate(o_halves, axis=1).astype(o_ref.dtype)
            lse_ref[r, j * bq:(j + 1) * bq, :] = jnp.concatenate(lse_halves, axis=1)


def _init_bias_tables_b(bias_ref, slopes):
    bq, half = ATTN_BQ, B_HALF_WINDOW
    bk = bq + 2 * half
    shape = (B_GROUP * bq, B_KV_HEADS * bk)
    row = lax.broadcasted_iota(jnp.int32, shape, 0)
    col = lax.broadcasted_iota(jnp.int32, shape, 1)
    key = col % bk
    head = (col // bk) * B_GROUP + row // bq
    rel = jnp.abs(key - half - (row % bq))
    slope = jnp.full(shape, slopes[0], _F32)
    for h in range(1, B_Q_HEADS):
        slope = jnp.where(head == h, slopes[h], slope)
    bias = -slope * LOG2_E * rel.astype(_F32)
    for t, (lo, hi) in enumerate(((0, bk), (half, bk), (0, bq + half))):
        bias_ref[t] = jnp.where((rel <= half) & (key >= lo) & (key < hi), bias, MASK_VALUE)


def _attn_b_kernel(q_ref, kp_ref, ko_ref, kn_ref, vp_ref, vo_ref, vn_ref, sink_ref, o_ref, bias_ref,
                   *, slopes, n_steps):
    step = pl.program_id(1)
    bq = ATTN_BQ
    bk = bq + 2 * B_HALF_WINDOW
    n_sub = q_ref.shape[0] // bq

    @pl.when((pl.program_id(0) == 0) & (step == 0))
    def _():
        _init_bias_tables_b(bias_ref, slopes)

    k_all = jnp.concatenate([kp_ref[...], ko_ref[...], kn_ref[...]], axis=0)
    v_all = jnp.concatenate([vp_ref[...], vo_ref[...], vn_ref[...]], axis=0)
    lane_kv = lax.broadcasted_iota(jnp.int32, (bq, B_KV), 1) // HEAD_DIM
    key_lane_kv = lax.broadcasted_iota(jnp.int32, k_all.shape, 1) // HEAD_DIM
    zero = jnp.zeros_like(k_all)
    k_kv = [jnp.where(key_lane_kv == kv, k_all, zero) for kv in range(B_KV_HEADS)]
    v_kv = [jnp.concatenate([jnp.where(key_lane_kv == kv, v_all, zero),
                             jnp.where(key_lane_kv == kv, 1.0, 0.0).astype(_BF16)], axis=1)
            for kv in range(B_KV_HEADS)]
    for j in range(n_sub):
        keys = slice(j * bq, j * bq + bk)
        qs = jnp.concatenate([q_ref[j * bq:(j + 1) * bq, c * B_KV:(c + 1) * B_KV]
                              for c in range(B_GROUP)], axis=0)
        k = jnp.concatenate([k_kv[kv][keys] for kv in range(B_KV_HEADS)], axis=0)
        v = jnp.concatenate([v_kv[kv][keys] for kv in range(B_KV_HEADS)], axis=0)
        s = lax.dot_general(qs, k, (((1,), (1,)), ((), ())), preferred_element_type=_F32)
        s = s + bias_ref[_table_index(j, n_sub, step, n_steps)]
        ps, sink_terms = [], []
        for c in range(B_GROUP):
            p_c, e_c = [], []
            for kv in range(B_KV_HEADS):
                sink = sink_ref[kv * B_GROUP + c]
                s_h = s[c * bq:(c + 1) * bq, kv * bk:(kv + 1) * bk]
                m_h = jnp.maximum(jnp.max(s_h, axis=1, keepdims=True), sink)
                p_c.append(jnp.exp2(s_h - m_h).astype(_BF16))
                e_c.append(jnp.exp2(sink - m_h))
            ps.append(jnp.concatenate(p_c, axis=1))
            sink_terms.append(e_c)
        pv = jnp.dot(jnp.concatenate(ps, axis=0), v, preferred_element_type=_F32)
        for c in range(B_GROUP):
            rows = pv[c * bq:(c + 1) * bq]
            den = rows[:, B_KV:] + jnp.where(lane_kv == 0, sink_terms[c][0], sink_terms[c][1])
            o_ref[j * bq:(j + 1) * bq, c * B_KV:(c + 1) * B_KV] = (rows[:, :B_KV] / den).astype(o_ref.dtype)


def _natural_order(ref, dilation, slab_ref, base):
    n_slab = A_OUT // LANES
    if dilation == 1:
        x = ref[0].astype(_F32)
        return [x[:, s * LANES:(s + 1) * LANES] for s in range(n_slab)]
    rows = ref.shape[1]
    for r in range(dilation):
        blk = ref[r].astype(_F32)
        for s in range(n_slab):
            slab_ref[base + s, pl.ds(r, rows, stride=dilation), :] = blk[:, s * LANES:(s + 1) * LANES]
    return [slab_ref[base + s] for s in range(n_slab)]


def _mix_ffn_kernel(h_ref, u_ref, oa0_ref, la0_ref, oa1_ref, la1_ref, oa2_ref, la2_ref, yb_ref,
                    wgate_ref, bgate_ref, wpa_ref, wpb_ref, wout_ref,
                    g2_ref, wg_ref, wu_ref, wd_ref, gfin_ref, y_ref, acc_ref, slab_ref):
    n_slab = A_OUT // LANES
    outs, lses = [], []
    for g, (o_ref, l_ref) in enumerate(((oa0_ref, la0_ref), (oa1_ref, la1_ref), (oa2_ref, la2_ref))):
        d = A_PAIRS[g][1]
        outs.append(_natural_order(o_ref, d, slab_ref, 2 * g * n_slab))
        lses.append(_natural_order(l_ref, d, slab_ref, (2 * g + 1) * n_slab))
    ya = []
    for s in range(n_slab):
        la0, la1, la2 = lses[0][s], lses[1][s], lses[2][s]
        mx = jnp.maximum(jnp.maximum(la0, la1), la2)
        e0, e1, e2 = jnp.exp2(la0 - mx), jnp.exp2(la1 - mx), jnp.exp2(la2 - mx)
        ya.append((e0 * outs[0][s] + e1 * outs[1][s] + e2 * outs[2][s]) / (e0 + e1 + e2))
    ya = jnp.concatenate(ya, axis=1)
    h = h_ref[...]
    u = u_ref[...]
    t = jnp.tanh(jnp.dot(u, wgate_ref[...], preferred_element_type=_F32) + bgate_ref[...])
    pa = jnp.dot(ya.astype(_BF16), wpa_ref[...], preferred_element_type=_F32)
    pb = jnp.dot(yb_ref[...], wpb_ref[...], preferred_element_type=_F32)
    mixed = (pa + pb) + t[:, :D_MODEL] * pa + t[:, D_MODEL:] * pb
    h = h + jnp.dot(mixed.astype(_BF16), wout_ref[...], preferred_element_type=_F32)
    hn = _rms(h, g2_ref[...]).astype(_BF16)
    _swiglu_into(acc_ref, hn, wg_ref, wu_ref, wd_ref)
    h = h + 0.5 * acc_ref[...]
    y_ref[...] = _rms(h, gfin_ref[...])


def _resident(shape):
    nd = len(shape)
    return pl.BlockSpec(shape, lambda *_: (0,) * nd, pipeline_mode=pl.Buffered(1))


def _residue_spec(tm, d, width):
    return pl.BlockSpec((None, d, tm // d, width), lambda b, i: (b, 0, i, 0))


def _ffn_proj(x, p):
    batch, seq, _ = x.shape
    tm = TOKEN_TILE
    row = lambda w: pl.BlockSpec((None, tm, w), lambda b, i: (b, i, 0))
    consts = (p["g1"], p["wg1"], p["wu1"], p["wd1"], p["gmix"], p["win"], p["same_head"], p["gain"])
    dils = [d for _, d in A_PAIRS]
    return pl.pallas_call(
        _ffn_proj_kernel,
        grid=(batch, seq // tm),
        in_specs=[row(D_MODEL)] + [_resident(c.shape) for c in consts],
        out_specs=[row(D_MODEL)] * 2 + [_residue_spec(tm, d, GROUP_W) for d in dils] + [row(QKVB_W)],
        out_shape=[jax.ShapeDtypeStruct((batch, seq, D_MODEL), _F32),
                   jax.ShapeDtypeStruct((batch, seq, D_MODEL), _BF16)]
        + [jax.ShapeDtypeStruct((batch, d, seq // d, GROUP_W), _BF16) for d in dils]
        + [jax.ShapeDtypeStruct((batch, seq, QKVB_W), _BF16)],
        scratch_shapes=[pltpu.VMEM((tm, D_MODEL), _F32),
                        pltpu.VMEM((2 * GROUP_W // LANES, tm, LANES), _F32)],
        compiler_params=pltpu.CompilerParams(
            dimension_semantics=("arbitrary",) * 2, vmem_limit_bytes=V7X_VMEM_LIMIT_BYTES),
        name="ffn_proj",
    )(x, *consts)


def _attn_a(qkv, group):
    batch, d, length, _ = qkv.shape
    rows = min(length, ATTN_STEP_ROWS)
    planes = ATTN_STEP_ROWS // rows
    n_steps = length // rows
    halo = rows // A_HALF
    slopes = [float(2.0 ** (-8.0 * (group * A_SLOTS + h + 1) / A_HEADS)) for h in range(A_SLOTS)]

    def own(col):
        return pl.BlockSpec((None, planes, rows, A_OUT), lambda b, r, n: (b, r, n, col))

    def prev(col):
        return pl.BlockSpec((None, planes, A_HALF, A_OUT),
                            lambda b, r, n: (b, r, jnp.maximum(n * halo - 1, 0), col))

    def nxt(col):
        return pl.BlockSpec((None, planes, A_HALF, A_OUT),
                            lambda b, r, n: (b, r, jnp.minimum((n + 1) * halo, n_steps * halo - 1), col))

    return pl.pallas_call(
        functools.partial(_attn_a_kernel, dilation=d, slopes=slopes, n_steps=n_steps),
        grid=(batch, d // planes, n_steps),
        in_specs=[own(0), prev(1), own(1), nxt(1), prev(2), own(2), nxt(2)],
        out_specs=[own(0), own(0)],
        out_shape=[jax.ShapeDtypeStruct((batch, d, length, A_OUT), _BF16),
                   jax.ShapeDtypeStruct((batch, d, length, A_OUT), _F32)],
        scratch_shapes=[pltpu.VMEM((3, A_SLOTS * ATTN_BQ, ATTN_BQ + 2 * A_HALF), _F32)],
        compiler_params=pltpu.CompilerParams(dimension_semantics=("arbitrary",) * 3),
        name=f"attn_a{group}",
    )(qkv, qkv, qkv, qkv, qkv, qkv, qkv)


def _attn_b(qkvb, sink):
    batch, seq, _ = qkvb.shape
    rows = ATTN_STEP_ROWS
    n_steps = seq // rows
    halo = rows // B_HALF_WINDOW
    slopes = [float(2.0 ** (-8.0 * (h + 1) / B_Q_HEADS)) for h in range(B_Q_HEADS)]
    k_col = B_OUT // B_KV

    def own(col):
        return pl.BlockSpec((None, rows, B_KV), lambda b, n: (b, n, col))

    def prev(col):
        return pl.BlockSpec((None, B_HALF_WINDOW, B_KV),
                            lambda b, n: (b, jnp.maximum(n * halo - 1, 0), col))

    def nxt(col):
        return pl.BlockSpec((None, B_HALF_WINDOW, B_KV),
                            lambda b, n: (b, jnp.minimum((n + 1) * halo, n_steps * halo - 1), col))

    q_spec = pl.BlockSpec((None, rows, B_OUT), lambda b, n: (b, n, 0))
    return pl.pallas_call(
        functools.partial(_attn_b_kernel, slopes=slopes, n_steps=n_steps),
        grid=(batch, n_steps),
        in_specs=[q_spec, prev(k_col), own(k_col), nxt(k_col),
                  prev(k_col + 1), own(k_col + 1), nxt(k_col + 1),
                  pl.BlockSpec(memory_space=pltpu.SMEM)],
        out_specs=q_spec,
        out_shape=jax.ShapeDtypeStruct((batch, seq, B_OUT), _BF16),
        scratch_shapes=[pltpu.VMEM((3, B_GROUP * ATTN_BQ, B_KV_HEADS * (ATTN_BQ + 2 * B_HALF_WINDOW)), _F32)],
        compiler_params=pltpu.CompilerParams(dimension_semantics=("arbitrary",) * 2),
        name="attn_b",
    )(qkvb, qkvb, qkvb, qkvb, qkvb, qkvb, qkvb, sink)


def _mix_ffn(h, u, oa, la, yb, p):
    batch, seq, _ = h.shape
    tm = TOKEN_TILE
    row = lambda w: pl.BlockSpec((None, tm, w), lambda b, i: (b, i, 0))
    consts = (p["wgate"], p["bgate"], p["wpa"], p["wpb"], p["wout"],
              p["g2"], p["wg2"], p["wu2"], p["wd2"], p["gfin"])
    attn_specs, attn_args = [], []
    for g, (_, d) in enumerate(A_PAIRS):
        attn_specs += [_residue_spec(tm, d, A_OUT)] * 2
        attn_args += [oa[g], la[g]]
    return pl.pallas_call(
        _mix_ffn_kernel,
        grid=(batch, seq // tm),
        in_specs=[row(D_MODEL)] * 2 + attn_specs + [row(B_OUT)] + [_resident(c.shape) for c in consts],
        out_specs=row(D_MODEL),
        out_shape=jax.ShapeDtypeStruct((batch, seq, D_MODEL), _F32),
        scratch_shapes=[pltpu.VMEM((tm, D_MODEL), _F32),
                        pltpu.VMEM((2 * N_GROUPS * A_OUT // LANES, tm, LANES), _F32)],
        compiler_params=pltpu.CompilerParams(
            dimension_semantics=("arbitrary",) * 2, vmem_limit_bytes=V7X_VMEM_LIMIT_BYTES),
        name="mix_ffn",
    )(h, u, *attn_args, yb, *consts)


def _b_head_perm():
    idx = [(kv * B_GROUP + c) * HEAD_DIM + e
           for c in range(B_GROUP) for kv in range(B_KV_HEADS) for e in range(HEAD_DIM)]
    return np.asarray(idx, dtype=np.int32)


def _prepare_params(ffn1_norm, ffn1_w_gate, ffn1_w_up, ffn1_w_down, mix_norm, w_in, a_q_norm,
                    a_k_norm, b_q_norm, b_k_norm, b_sink, w_proj_a, w_proj_b, w_gate, b_gate, w_out,
                    ffn2_norm, ffn2_w_gate, ffn2_w_up, ffn2_w_down, final_norm):
    def vec(g):
        return g.reshape(1, -1).astype(_F32)

    def bf16(w):
        return w.astype(_BF16)

    def half(w):
        return (0.5 * w).astype(_BF16)

    perm = _b_head_perm()
    qa, ka, va, qb, kb, vb = jnp.split(w_in, np.cumsum([A_W, A_W, A_W, B_OUT, B_KV]).tolist(), axis=1)
    grp = lambda w, g: w[:, g * A_OUT:(g + 1) * A_OUT]
    chunks = [m for g in range(N_GROUPS) for m in (grp(qa, g), grp(ka, g), grp(va, g))] + [qb[:, perm], kb, vb]
    win = jnp.concatenate(chunks, axis=1).astype(_BF16)
    scale = HEAD_DIM ** -0.5 * LOG2_E
    ones = lambda n: jnp.ones((n,), _F32)
    gain = jnp.concatenate(
        [jnp.tile(a_q_norm * scale, A_SLOTS), jnp.tile(a_k_norm, A_SLOTS), ones(A_OUT)] * N_GROUPS
        + [jnp.tile(b_q_norm * scale, B_Q_HEADS), jnp.tile(b_k_norm, B_KV_HEADS), ones(B_KV)])
    head_of_col = np.arange(MXU_TILE) // HEAD_DIM
    same_head = jnp.asarray(head_of_col[:, None] == head_of_col[None, :], dtype=_BF16)
    return dict(
        g1=vec(ffn1_norm), wg1=half(ffn1_w_gate), wu1=bf16(ffn1_w_up), wd1=bf16(ffn1_w_down),
        gmix=vec(mix_norm), win=win, same_head=same_head, gain=vec(gain), sink=b_sink.astype(_F32) * LOG2_E,
        wgate=half(w_gate), bgate=vec(0.5 * b_gate), wpa=w_proj_a.astype(_BF16),
        wpb=w_proj_b[perm, :].astype(_BF16), wout=half(w_out),
        g2=vec(ffn2_norm), wg2=half(ffn2_w_gate), wu2=bf16(ffn2_w_up), wd2=bf16(ffn2_w_down),
        gfin=vec(final_norm),
    )


def _encoder(x, p):
    h, u, a0, a1, a2, qkvb = _ffn_proj(x, p)
    oa, la = zip(*[_attn_a(a, g) for g, a in enumerate((a0, a1, a2))])
    yb = _attn_b(qkvb, p["sink"])
    return _mix_ffn(h, u, oa, la, yb, p)


def kernel(x_prompt, x_sample, ffn1_norm, ffn1_w_gate, ffn1_w_up, ffn1_w_down, mix_norm, w_in, a_q_norm, a_k_norm, b_q_norm, b_k_norm, b_sink, w_proj_a, w_proj_b, w_gate, b_gate, w_out, ffn2_norm, ffn2_w_gate, ffn2_w_up, ffn2_w_down, final_norm):
    p = _prepare_params(*[w[0] for w in (
        ffn1_norm, ffn1_w_gate, ffn1_w_up, ffn1_w_down, mix_norm, w_in, a_q_norm, a_k_norm, b_q_norm,
        b_k_norm, b_sink, w_proj_a, w_proj_b, w_gate, b_gate, w_out, ffn2_norm, ffn2_w_gate, ffn2_w_up,
        ffn2_w_down, final_norm)])
    return (_encoder(x_prompt, p), _encoder(x_sample, p))
```

```python
import functools

import numpy as np
import jax
import jax.numpy as jnp
from jax import lax
from jax.experimental import pallas as pl
from jax.experimental.pallas import tpu as pltpu

D_MODEL = 1024
D_FF = 2816
HEAD_DIM = 64
A_PAIRS = ((128, 1), (512, 4), (2048, 16))
N_GROUPS = len(A_PAIRS)
A_SLOTS = 4
A_HEADS = A_SLOTS * N_GROUPS
A_OUT = A_SLOTS * HEAD_DIM
A_W = A_HEADS * HEAD_DIM
B_Q_HEADS = 8
B_KV_HEADS = 2
B_GROUP = B_Q_HEADS // B_KV_HEADS
B_HALF_WINDOW = 128
B_OUT = B_Q_HEADS * HEAD_DIM
B_KV = B_KV_HEADS * HEAD_DIM
NORM_EPS = 1e-6
MASK_VALUE = -1e30
LOG2_E = 1.4426950408889634

MXU_TILE = 256
GROUP_W = 3 * A_OUT
QKVB_W = B_OUT + 2 * B_KV

LANES = 128
FF_CHUNKS = (512,) * 5 + (256,)
TOKEN_TILE = 512
ATTN_BQ = 128
ATTN_STEP_ROWS = 4096
A_HALF = 64
V7X_VMEM_LIMIT_BYTES = 56 * 1024 * 1024

_F32 = jnp.float32
_BF16 = jnp.bfloat16


def _rms(x, g):
    ms = jnp.mean(x * x, axis=-1, keepdims=True)
    return x * lax.rsqrt(ms + NORM_EPS) * g


def _swiglu_into(acc_ref, xn, wg_half_ref, wu_ref, wd_ref):
    lo = 0
    for w in FF_CHUNKS:
        g = jnp.dot(xn, wg_half_ref[:, lo:lo + w], preferred_element_type=_F32)
        u = jnp.dot(xn, wu_ref[:, lo:lo + w], preferred_element_type=_F32)
        a = (g * (1.0 + jnp.tanh(g)) * u).astype(_BF16)
        part = jnp.dot(a, wd_ref[lo:lo + w, :], preferred_element_type=_F32)
        if lo == 0:
            acc_ref[...] = part
        else:
            acc_ref[...] += part
        lo += w


def _qk_normed(chunk, n_norm, same_head_ref, gain):
    parts = []
    for lo in range(0, n_norm, MXU_TILE):
        w = min(MXU_TILE, n_norm - lo)
        x = chunk[:, lo:lo + w]
        ssq = jnp.dot((x * x).astype(_BF16), same_head_ref[:w, :w], preferred_element_type=_F32)
        parts.append(x * lax.rsqrt(ssq * (1.0 / HEAD_DIM) + NORM_EPS))
    return jnp.concatenate(parts, axis=1) * gain, chunk[:, n_norm:]


def _ffn_proj_kernel(x_ref, g1_ref, wg_ref, wu_ref, wd_ref, gmix_ref, win_ref, same_head_ref,
                     gain_ref, h_ref, u_ref, a0_ref, a1_ref, a2_ref, qkvb_ref, acc_ref, slab_ref):
    tm = x_ref.shape[0]
    x = x_ref[...]
    xn = _rms(x, g1_ref[...]).astype(_BF16)
    _swiglu_into(acc_ref, xn, wg_ref, wu_ref, wd_ref)
    h = x + 0.5 * acc_ref[...]
    h_ref[...] = h
    u = _rms(h, gmix_ref[...]).astype(_BF16)
    u_ref[...] = u
    n_slab = GROUP_W // LANES

    def project_group(g, a_ref):
        lo = g * GROUP_W
        chunk = jnp.dot(u, win_ref[:, lo:lo + GROUP_W], preferred_element_type=_F32)
        nq, val = _qk_normed(chunk, 2 * A_OUT, same_head_ref, gain_ref[:, lo:lo + 2 * A_OUT])
        d = A_PAIRS[g][1]
        if d == 1:
            a_ref[0, :, :2 * A_OUT] = nq.astype(_BF16)
            a_ref[0, :, 2 * A_OUT:] = val.astype(_BF16)
            return
        base = (g - 1) * n_slab
        for s in range(n_slab):
            src = nq if s * LANES < 2 * A_OUT else val
            col = s * LANES % (2 * A_OUT)
            slab_ref[base + s] = src[:, col:col + LANES]
        for r in range(d):
            for s in range(n_slab):
                a_ref[r, :, s * LANES:(s + 1) * LANES] = (
                    slab_ref[base + s, pl.ds(r, tm // d, stride=d), :].astype(_BF16))

    def project_b():
        lo = N_GROUPS * GROUP_W
        chunk = jnp.dot(u, win_ref[:, lo:lo + QKVB_W], preferred_element_type=_F32)
        nq, val = _qk_normed(chunk, B_OUT + B_KV, same_head_ref, gain_ref[:, lo:lo + B_OUT + B_KV])
        qkvb_ref[:, :B_OUT + B_KV] = nq.astype(_BF16)
        qkvb_ref[:, B_OUT + B_KV:] = val.astype(_BF16)

    project_group(2, a2_ref)
    project_group(1, a1_ref)
    project_group(0, a0_ref)
    project_b()


def _band_bias(rows_per_head, n_heads, bk, half, slopes, dist_scale, key_lo, key_hi):
    shape = (n_heads * rows_per_head, bk)
    row = lax.broadcasted_iota(jnp.int32, shape, 0)
    col = lax.broadcasted_iota(jnp.int32, shape, 1)
    rel = jnp.abs(col - half - (row % rows_per_head))
    head = row // rows_per_head
    slope = jnp.full(shape, slopes[0], _F32)
    for h in range(1, n_heads):
        slope = jnp.where(head == h, slopes[h], slope)
    dist = (rel * dist_scale).astype(_F32)
    ok = (rel <= half) & (col >= key_lo) & (col < key_hi)
    return jnp.where(ok, -slope * LOG2_E * dist, MASK_VALUE)


def _init_bias_tables(bias_ref, rows_per_head, n_heads, half, slopes, dist_scale):
    bk = rows_per_head + 2 * half
    for t, (lo, hi) in enumerate(((0, bk), (half, bk), (0, rows_per_head + half))):
        bias_ref[t] = _band_bias(rows_per_head, n_heads, bk, half, slopes, dist_scale, lo, hi)


def _table_index(j, n_sub, step, n_steps):
    if j == 0:
        return jnp.where(step == 0, 1, 0)
    if j == n_sub - 1:
        return jnp.where(step == n_steps - 1, 2, 0)
    return 0


def _attn_a_kernel(q_ref, kp_ref, ko_ref, kn_ref, vp_ref, vo_ref, vn_ref, o_ref, lse_ref, bias_ref,
                   *, dilation, slopes, n_steps):
    step = pl.program_id(2)
    bq = ATTN_BQ
    bk = bq + 2 * A_HALF
    planes, rows, _ = q_ref.shape
    n_sub = rows // bq

    @pl.when((pl.program_id(0) == 0) & (pl.program_id(1) == 0) & (step == 0))
    def _():
        _init_bias_tables(bias_ref, bq, A_SLOTS, A_HALF, slopes, dilation)

    first_of_pair = lax.broadcasted_iota(jnp.int32, (bq, LANES), 1) < HEAD_DIM
    for r in range(planes):
        k_all = jnp.concatenate([kp_ref[r], ko_ref[r], kn_ref[r]], axis=0)
        v_all = jnp.concatenate([vp_ref[r], vo_ref[r], vn_ref[r]], axis=0)
        for j in range(n_sub):
            zero = jnp.zeros((bq, LANES), _BF16)
            blocks = []
            for h in range(A_SLOTS):
                q_half = q_ref[r, j * bq:(j + 1) * bq, (h // 2) * LANES:(h // 2 + 1) * LANES]
                own = jnp.where(first_of_pair == (h % 2 == 0), q_half, zero)
                blocks.append(jnp.concatenate([own, zero] if h < 2 else [zero, own], axis=1))
            qs = jnp.concatenate(blocks, axis=0)
            k = k_all[j * bq:j * bq + bk]
            v = v_all[j * bq:j * bq + bk]
            s = lax.dot_general(qs, k, (((1,), (1,)), ((), ())), preferred_element_type=_F32)
            s = s + bias_ref[_table_index(j, n_sub, step, n_steps)]
            m = jnp.max(s, axis=1, keepdims=True)
            p = jnp.exp2(s - m)
            l = jnp.sum(p, axis=1, keepdims=True)
            pv = jnp.dot(p.astype(_BF16), v, preferred_element_type=_F32)
            inv_l = 1.0 / l
            lse2 = m + jnp.log2(l)
            o_halves, lse_halves = [], []
            for pair in range(A_SLOTS // 2):
                lanes = slice(pair * LANES, (pair + 1) * LANES)
                r0 = slice(2 * pair * bq, (2 * pair + 1) * bq)
                r1 = slice((2 * pair + 1) * bq, (2 * pair + 2) * bq)
                o_halves.append(jnp.where(first_of_pair, pv[r0, lanes] * inv_l[r0], pv[r1, lanes] * inv_l[r1]))
                lse_halves.append(jnp.where(first_of_pair, jnp.broadcast_to(lse2[r0], (bq, LANES)),
                                            jnp.broadcast_to(lse2[r1], (bq, LANES))))
            o_ref[r, j * bq:(j + 1) * bq, :] = jnp.concatenate(o_halves, axis=1).astype(o_ref.dtype)
            lse_ref[r, j * bq:(j + 1) * bq, :] = jnp.concatenate(lse_halves, axis=1)


def _init_bias_tables_b(bias_ref, slopes):
    bq, half = ATTN_BQ, B_HALF_WINDOW
    bk = bq + 2 * half
    shape = (B_GROUP * bq, B_KV_HEADS * bk)
    row = lax.broadcasted_iota(jnp.int32, shape, 0)
    col = lax.broadcasted_iota(jnp.int32, shape, 1)
    key = col % bk
    head = (col // bk) * B_GROUP + row // bq
    rel = jnp.abs(key - half - (row % bq))
    slope = jnp.full(shape, slopes[0], _F32)
    for h in range(1, B_Q_HEADS):
        slope = jnp.where(head == h, slopes[h], slope)
    bias = -slope * LOG2_E * rel.astype(_F32)
    for t, (lo, hi) in enumerate(((0, bk), (half, bk), (0, bq + half))):
        bias_ref[t] = jnp.where((rel <= half) & (key >= lo) & (key < hi), bias, MASK_VALUE)


def _attn_b_kernel(q_ref, kp_ref, ko_ref, kn_ref, vp_ref, vo_ref, vn_ref, sink_ref, o_ref, bias_ref,
                   *, slopes, n_steps):
    step = pl.program_id(1)
    bq = ATTN_BQ
    bk = bq + 2 * B_HALF_WINDOW
    n_sub = q_ref.shape[0] // bq

    @pl.when((pl.program_id(0) == 0) & (step == 0))
    def _():
        _init_bias_tables_b(bias_ref, slopes)

    k_all = jnp.concatenate([kp_ref[...], ko_ref[...], kn_ref[...]], axis=0)
    v_all = jnp.concatenate([vp_ref[...], vo_ref[...], vn_ref[...]], axis=0)
    lane_kv = lax.broadcasted_iota(jnp.int32, (bq, B_KV), 1) // HEAD_DIM
    key_lane_kv = lax.broadcasted_iota(jnp.int32, k_all.shape, 1) // HEAD_DIM
    zero = jnp.zeros_like(k_all)
    k_kv = [jnp.where(key_lane_kv == kv, k_all, zero) for kv in range(B_KV_HEADS)]
    v_kv = [jnp.concatenate([jnp.where(key_lane_kv == kv, v_all, zero),
                             jnp.where(key_lane_kv == kv, 1.0, 0.0).astype(_BF16)], axis=1)
            for kv in range(B_KV_HEADS)]
    for j in range(n_sub):
        keys = slice(j * bq, j * bq + bk)
        qs = jnp.concatenate([q_ref[j * bq:(j + 1) * bq, c * B_KV:(c + 1) * B_KV]
                              for c in range(B_GROUP)], axis=0)
        k = jnp.concatenate([k_kv[kv][keys] for kv in range(B_KV_HEADS)], axis=0)
        v = jnp.concatenate([v_kv[kv][keys] for kv in range(B_KV_HEADS)], axis=0)
        s = lax.dot_general(qs, k, (((1,), (1,)), ((), ())), preferred_element_type=_F32)
        s = s + bias_ref[_table_index(j, n_sub, step, n_steps)]
        ps, sink_terms = [], []
        for c in range(B_GROUP):
            p_c, e_c = [], []
            for kv in range(B_KV_HEADS):
                sink = sink_ref[kv * B_GROUP + c]
                s_h = s[c * bq:(c + 1) * bq, kv * bk:(kv + 1) * bk]
                m_h = jnp.maximum(jnp.max(s_h, axis=1, keepdims=True), sink)
                p_c.append(jnp.exp2(s_h - m_h).astype(_BF16))
                e_c.append(jnp.exp2(sink - m_h))
            ps.append(jnp.concatenate(p_c, axis=1))
            sink_terms.append(e_c)
        pv = jnp.dot(jnp.concatenate(ps, axis=0), v, preferred_element_type=_F32)
        for c in range(B_GROUP):
            rows = pv[c * bq:(c + 1) * bq]
            den = rows[:, B_KV:] + jnp.where(lane_kv == 0, sink_terms[c][0], sink_terms[c][1])
            o_ref[j * bq:(j + 1) * bq, c * B_KV:(c + 1) * B_KV] = (rows[:, :B_KV] / den).astype(o_ref.dtype)


def _natural_order(ref, dilation, slab_ref, base):
    n_slab = A_OUT // LANES
    if dilation == 1:
        x = ref[0].astype(_F32)
        return [x[:, s * LANES:(s + 1) * LANES] for s in range(n_slab)]
    rows = ref.shape[1]
    for r in range(dilation):
        blk = ref[r].astype(_F32)
        for s in range(n_slab):
            slab_ref[base + s, pl.ds(r, rows, stride=dilation), :] = blk[:, s * LANES:(s + 1) * LANES]
    return [slab_ref[base + s] for s in range(n_slab)]


def _mix_ffn_kernel(h_ref, u_ref, oa0_ref, la0_ref, oa1_ref, la1_ref, oa2_ref, la2_ref, yb_ref,
                    wgate_ref, bgate_ref, wpa_ref, wpb_ref, wout_ref,
                    g2_ref, wg_ref, wu_ref, wd_ref, gfin_ref, y_ref, acc_ref, slab_ref):
    n_slab = A_OUT // LANES
    outs, lses = [], []
    for g, (o_ref, l_ref) in enumerate(((oa0_ref, la0_ref), (oa1_ref, la1_ref), (oa2_ref, la2_ref))):
        d = A_PAIRS[g][1]
        outs.append(_natural_order(o_ref, d, slab_ref, 2 * g * n_slab))
        lses.append(_natural_order(l_ref, d, slab_ref, (2 * g + 1) * n_slab))
    ya = []
    for s in range(n_slab):
        la0, la1, la2 = lses[0][s], lses[1][s], lses[2][s]
        mx = jnp.maximum(jnp.maximum(la0, la1), la2)
        e0, e1, e2 = jnp.exp2(la0 - mx), jnp.exp2(la1 - mx), jnp.exp2(la2 - mx)
        ya.append((e0 * outs[0][s] + e1 * outs[1][s] + e2 * outs[2][s]) / (e0 + e1 + e2))
    ya = jnp.concatenate(ya, axis=1)
    h = h_ref[...]
    u = u_ref[...]
    t = jnp.tanh(jnp.dot(u, wgate_ref[...], preferred_element_type=_F32) + bgate_ref[...])
    pa = jnp.dot(ya.astype(_BF16), wpa_ref[...], preferred_element_type=_F32)
    pb = jnp.dot(yb_ref[...], wpb_ref[...], preferred_element_type=_F32)
    mixed = (pa + pb) + t[:, :D_MODEL] * pa + t[:, D_MODEL:] * pb
    h = h + jnp.dot(mixed.astype(_BF16), wout_ref[...], preferred_element_type=_F32)
    hn = _rms(h, g2_ref[...]).astype(_BF16)
    _swiglu_into(acc_ref, hn, wg_ref, wu_ref, wd_ref)
    h = h + 0.5 * acc_ref[...]
    y_ref[...] = _rms(h, gfin_ref[...])


def _resident(shape):
    nd = len(shape)
    return pl.BlockSpec(shape, lambda *_: (0,) * nd, pipeline_mode=pl.Buffered(1))


def _residue_spec(tm, d, width):
    return pl.BlockSpec((None, d, tm // d, width), lambda b, i: (b, 0, i, 0))


def _ffn_proj(x, p):
    batch, seq, _ = x.shape
    tm = TOKEN_TILE
    row = lambda w: pl.BlockSpec((None, tm, w), lambda b, i: (b, i, 0))
    consts = (p["g1"], p["wg1"], p["wu1"], p["wd1"], p["gmix"], p["win"], p["same_head"], p["gain"])
    dils = [d for _, d in A_PAIRS]
    return pl.pallas_call(
        _ffn_proj_kernel,
        grid=(batch, seq // tm),
        in_specs=[row(D_MODEL)] + [_resident(c.shape) for c in consts],
        out_specs=[row(D_MODEL)] * 2 + [_residue_spec(tm, d, GROUP_W) for d in dils] + [row(QKVB_W)],
        out_shape=[jax.ShapeDtypeStruct((batch, seq, D_MODEL), _F32),
                   jax.ShapeDtypeStruct((batch, seq, D_MODEL), _BF16)]
        + [jax.ShapeDtypeStruct((batch, d, seq // d, GROUP_W), _BF16) for d in dils]
        + [jax.ShapeDtypeStruct((batch, seq, QKVB_W), _BF16)],
        scratch_shapes=[pltpu.VMEM((tm, D_MODEL), _F32),
                        pltpu.VMEM((2 * GROUP_W // LANES, tm, LANES), _F32)],
        compiler_params=pltpu.CompilerParams(
            dimension_semantics=("arbitrary",) * 2, vmem_limit_bytes=V7X_VMEM_LIMIT_BYTES),
        name="ffn_proj",
    )(x, *consts)


def _attn_a(qkv, group):
    batch, d, length, _ = qkv.shape
    rows = min(length, ATTN_STEP_ROWS)
    planes = ATTN_STEP_ROWS // rows
    n_steps = length // rows
    halo = rows // A_HALF
    slopes = [float(2.0 ** (-8.0 * (group * A_SLOTS + h + 1) / A_HEADS)) for h in range(A_SLOTS)]

    def own(col):
        return pl.BlockSpec((None, planes, rows, A_OUT), lambda b, r, n: (b, r, n, col))

    def prev(col):
        return pl.BlockSpec((None, planes, A_HALF, A_OUT),
                            lambda b, r, n: (b, r, jnp.maximum(n * halo - 1, 0), col))

    def nxt(col):
        return pl.BlockSpec((None, planes, A_HALF, A_OUT),
                            lambda b, r, n: (b, r, jnp.minimum((n + 1) * halo, n_steps * halo - 1), col))

    return pl.pallas_call(
        functools.partial(_attn_a_kernel, dilation=d, slopes=slopes, n_steps=n_steps),
        grid=(batch, d // planes, n_steps),
        in_specs=[own(0), prev(1), own(1), nxt(1), prev(2), own(2), nxt(2)],
        out_specs=[own(0), own(0)],
        out_shape=[jax.ShapeDtypeStruct((batch, d, length, A_OUT), _BF16),
                   jax.ShapeDtypeStruct((batch, d, length, A_OUT), _F32)],
        scratch_shapes=[pltpu.VMEM((3, A_SLOTS * ATTN_BQ, ATTN_BQ + 2 * A_HALF), _F32)],
        compiler_params=pltpu.CompilerParams(dimension_semantics=("arbitrary",) * 3),
        name=f"attn_a{group}",
    )(qkv, qkv, qkv, qkv, qkv, qkv, qkv)


def _attn_b(qkvb, sink):
    batch, seq, _ = qkvb.shape
    rows = ATTN_STEP_ROWS
    n_steps = seq // rows
    halo = rows // B_HALF_WINDOW
    slopes = [float(2.0 ** (-8.0 * (h + 1) / B_Q_HEADS)) for h in range(B_Q_HEADS)]
    k_col = B_OUT // B_KV

    def own(col):
        return pl.BlockSpec((None, rows, B_KV), lambda b, n: (b, n, col))

    def prev(col):
        return pl.BlockSpec((None, B_HALF_WINDOW, B_KV),
                            lambda b, n: (b, jnp.maximum(n * halo - 1, 0), col))

    def nxt(col):
        return pl.BlockSpec((None, B_HALF_WINDOW, B_KV),
                            lambda b, n: (b, jnp.minimum((n + 1) * halo, n_steps * halo - 1), col))

    q_spec = pl.BlockSpec((None, rows, B_OUT), lambda b, n: (b, n, 0))
    return pl.pallas_call(
        functools.partial(_attn_b_kernel, slopes=slopes, n_steps=n_steps),
        grid=(batch, n_steps),
        in_specs=[q_spec, prev(k_col), own(k_col), nxt(k_col),
                  prev(k_col + 1), own(k_col + 1), nxt(k_col + 1),
                  pl.BlockSpec(memory_space=pltpu.SMEM)],
        out_specs=q_spec,
        out_shape=jax.ShapeDtypeStruct((batch, seq, B_OUT), _BF16),
        scratch_shapes=[pltpu.VMEM((3, B_GROUP * ATTN_BQ, B_KV_HEADS * (ATTN_BQ + 2 * B_HALF_WINDOW)), _F32)],
        compiler_params=pltpu.CompilerParams(dimension_semantics=("arbitrary",) * 2),
        name="attn_b",
    )(qkvb, qkvb, qkvb, qkvb, qkvb, qkvb, qkvb, sink)


def _mix_ffn(h, u, oa, la, yb, p):
    batch, seq, _ = h.shape
    tm = TOKEN_TILE
    row = lambda w: pl.BlockSpec((None, tm, w), lambda b, i: (b, i, 0))
    consts = (p["wgate"], p["bgate"], p["wpa"], p["wpb"], p["wout"],
              p["g2"], p["wg2"], p["wu2"], p["wd2"], p["gfin"])
    attn_specs, attn_args = [], []
    for g, (_, d) in enumerate(A_PAIRS):
        attn_specs += [_residue_spec(tm, d, A_OUT)] * 2
        attn_args += [oa[g], la[g]]
    return pl.pallas_call(
        _mix_ffn_kernel,
        grid=(batch, seq // tm),
        in_specs=[row(D_MODEL)] * 2 + attn_specs + [row(B_OUT)] + [_resident(c.shape) for c in consts],
        out_specs=row(D_MODEL),
        out_shape=jax.ShapeDtypeStruct((batch, seq, D_MODEL), _F32),
        scratch_shapes=[pltpu.VMEM((tm, D_MODEL), _F32),
                        pltpu.VMEM((2 * N_GROUPS * A_OUT // LANES, tm, LANES), _F32)],
        compiler_params=pltpu.CompilerParams(
            dimension_semantics=("arbitrary",) * 2, vmem_limit_bytes=V7X_VMEM_LIMIT_BYTES),
        name="mix_ffn",
    )(h, u, *attn_args, yb, *consts)


def _b_head_perm():
    idx = [(kv * B_GROUP + c) * HEAD_DIM + e
           for c in range(B_GROUP) for kv in range(B_KV_HEADS) for e in range(HEAD_DIM)]
    return np.asarray(idx, dtype=np.int32)


def _prepare_params(ffn1_norm, ffn1_w_gate, ffn1_w_up, ffn1_w_down, mix_norm, w_in, a_q_norm,
                    a_k_norm, b_q_norm, b_k_norm, b_sink, w_proj_a, w_proj_b, w_gate, b_gate, w_out,
                    ffn2_norm, ffn2_w_gate, ffn2_w_up, ffn2_w_down, final_norm):
    def vec(g):
        return g.reshape(1, -1).astype(_F32)

    def bf16(w):
        return w.astype(_BF16)

    def half(w):
        return (0.5 * w).astype(_BF16)

    perm = _b_head_perm()
    qa, ka, va, qb, kb, vb = jnp.split(w_in, np.cumsum([A_W, A_W, A_W, B_OUT, B_KV]).tolist(), axis=1)
    grp = lambda w, g: w[:, g * A_OUT:(g + 1) * A_OUT]
    chunks = [m for g in range(N_GROUPS) for m in (grp(qa, g), grp(ka, g), grp(va, g))] + [qb[:, perm], kb, vb]
    win = jnp.concatenate(chunks, axis=1).astype(_BF16)
    scale = HEAD_DIM ** -0.5 * LOG2_E
    ones = lambda n: jnp.ones((n,), _F32)
    gain = jnp.concatenate(
        [jnp.tile(a_q_norm * scale, A_SLOTS), jnp.tile(a_k_norm, A_SLOTS), ones(A_OUT)] * N_GROUPS
        + [jnp.tile(b_q_norm * scale, B_Q_HEADS), jnp.tile(b_k_norm, B_KV_HEADS), ones(B_KV)])
    head_of_col = np.arange(MXU_TILE) // HEAD_DIM
    same_head = jnp.asarray(head_of_col[:, None] == head_of_col[None, :], dtype=_BF16)
    return dict(
        g1=vec(ffn1_norm), wg1=half(ffn1_w_gate), wu1=bf16(ffn1_w_up), wd1=bf16(ffn1_w_down),
        gmix=vec(mix_norm), win=win, same_head=same_head, gain=vec(gain), sink=b_sink.astype(_F32) * LOG2_E,
        wgate=half(w_gate), bgate=vec(0.5 * b_gate), wpa=w_proj_a.astype(_BF16),
        wpb=w_proj_b[perm, :].astype(_BF16), wout=half(w_out),
        g2=vec(ffn2_norm), wg2=half(ffn2_w_gate), wu2=bf16(ffn2_w_up), wd2=bf16(ffn2_w_down),
        gfin=vec(final_norm),
    )


def _encoder(x, p):
    h, u, a0, a1, a2, qkvb = _ffn_proj(x, p)
    oa, la = zip(*[_attn_a(a, g) for g, a in enumerate((a0, a1, a2))])
    yb = _attn_b(qkvb, p["sink"])
    return _mix_ffn(h, u, oa, la, yb, p)


def kernel(x_prompt, x_sample, ffn1_norm, ffn1_w_gate, ffn1_w_up, ffn1_w_down, mix_norm, w_in, a_q_norm, a_k_norm, b_q_norm, b_k_norm, b_sink, w_proj_a, w_proj_b, w_gate, b_gate, w_out, ffn2_norm, ffn2_w_gate, ffn2_w_up, ffn2_w_down, final_norm):
    p = _prepare_params(*[w[0] for w in (
        ffn1_norm, ffn1_w_gate, ffn1_w_up, ffn1_w_down, mix_norm, w_in, a_q_norm, a_k_norm, b_q_norm,
        b_k_norm, b_sink, w_proj_a, w_proj_b, w_gate, b_gate, w_out, ffn2_norm, ffn2_w_gate, ffn2_w_up,
        ffn2_w_down, final_norm)])
    return (_encoder(x_prompt, p), _encoder(x_sample, p))
```

```python
import functools

import numpy as np
import jax
import jax.numpy as jnp
from jax import lax
from jax.experimental import pallas as pl
from jax.experimental.pallas import tpu as pltpu

D_MODEL = 1024
D_FF = 2816
HEAD_DIM = 64
A_PAIRS = ((128, 1), (512, 4), (2048, 16))
N_GROUPS = len(A_PAIRS)
A_SLOTS = 4
A_HEADS = A_SLOTS * N_GROUPS
A_OUT = A_SLOTS * HEAD_DIM
A_W = A_HEADS * HEAD_DIM
B_Q_HEADS = 8
B_KV_HEADS = 2
B_GROUP = B_Q_HEADS // B_KV_HEADS
B_HALF_WINDOW = 128
B_OUT = B_Q_HEADS * HEAD_DIM
B_KV = B_KV_HEADS * HEAD_DIM
NORM_EPS = 1e-6
MASK_VALUE = -1e30
LOG2_E = 1.4426950408889634

MXU_TILE = 256
GROUP_W = 3 * A_OUT
QKVB_W = B_OUT + 2 * B_KV

LANES = 128
STRIDE_FAST = 4
FF_CHUNKS = (512,) * 5 + (256,)
TOKEN_TILE = 512
ATTN_BQ = 128
ATTN_STEP_ROWS = 4096
A_HALF = 64
V7X_VMEM_LIMIT_BYTES = 56 * 1024 * 1024

_F32 = jnp.float32
_BF16 = jnp.bfloat16


def _rms(x, g):
    ms = jnp.mean(x * x, axis=-1, keepdims=True)
    return x * lax.rsqrt(ms + NORM_EPS) * g


def _swiglu_into(acc_ref, xn, wg_half_ref, wu_ref, wd_ref):
    lo = 0
    for w in FF_CHUNKS:
        g = jnp.dot(xn, wg_half_ref[:, lo:lo + w], preferred_element_type=_F32)
        u = jnp.dot(xn, wu_ref[:, lo:lo + w], preferred_element_type=_F32)
        a = (g * (1.0 + jnp.tanh(g)) * u).astype(_BF16)
        part = jnp.dot(a, wd_ref[lo:lo + w, :], preferred_element_type=_F32)
        if lo == 0:
            acc_ref[...] = part
        else:
            acc_ref[...] += part
        lo += w


def _qk_normed(chunk, n_norm, same_head_ref, gain):
    parts = []
    for lo in range(0, n_norm, MXU_TILE):
        w = min(MXU_TILE, n_norm - lo)
        x = chunk[:, lo:lo + w]
        ssq = jnp.dot((x * x).astype(_BF16), same_head_ref[:w, :w], preferred_element_type=_F32)
        parts.append(x * lax.rsqrt(ssq * (1.0 / HEAD_DIM) + NORM_EPS))
    return jnp.concatenate(parts, axis=1) * gain, chunk[:, n_norm:]


def _ffn_proj_kernel(x_ref, g1_ref, wg_ref, wu_ref, wd_ref, gmix_ref, win_ref, same_head_ref,
                     gain_ref, h_ref, u_ref, a0_ref, a1_ref, a2_ref, qkvb_ref, acc_ref, slab_ref):
    tm = x_ref.shape[0]
    x = x_ref[...]
    xn = _rms(x, g1_ref[...]).astype(_BF16)
    _swiglu_into(acc_ref, xn, wg_ref, wu_ref, wd_ref)
    h = x + 0.5 * acc_ref[...]
    h_ref[...] = h
    u = _rms(h, gmix_ref[...]).astype(_BF16)
    u_ref[...] = u
    n_slab = GROUP_W // LANES

    def project_group(g, a_ref):
        lo = g * GROUP_W
        chunk = jnp.dot(u, win_ref[:, lo:lo + GROUP_W], preferred_element_type=_F32)
        nq, val = _qk_normed(chunk, 2 * A_OUT, same_head_ref, gain_ref[:, lo:lo + 2 * A_OUT])
        d = A_PAIRS[g][1]
        if d == 1:
            a_ref[0, :, :2 * A_OUT] = nq.astype(_BF16)
            a_ref[0, :, 2 * A_OUT:] = val.astype(_BF16)
            return
        base = (g - 1) * n_slab
        for s in range(n_slab):
            src = nq if s * LANES < 2 * A_OUT else val
            col = s * LANES % (2 * A_OUT)
            slab_ref[base + s] = src[:, col:col + LANES]
        for r in range(d):
            for s in range(n_slab):
                a_ref[r, :, s * LANES:(s + 1) * LANES] = (
                    slab_ref[base + s, pl.ds(r, tm // d, stride=d), :].astype(_BF16))

    def project_b():
        lo = N_GROUPS * GROUP_W
        chunk = jnp.dot(u, win_ref[:, lo:lo + QKVB_W], preferred_element_type=_F32)
        nq, val = _qk_normed(chunk, B_OUT + B_KV, same_head_ref, gain_ref[:, lo:lo + B_OUT + B_KV])
        qkvb_ref[:, :B_OUT + B_KV] = nq.astype(_BF16)
        qkvb_ref[:, B_OUT + B_KV:] = val.astype(_BF16)

    project_group(2, a2_ref)
    project_group(1, a1_ref)
    project_group(0, a0_ref)
    project_b()


def _band_bias(rows_per_head, n_heads, bk, half, slopes, dist_scale, key_lo, key_hi):
    shape = (n_heads * rows_per_head, bk)
    row = lax.broadcasted_iota(jnp.int32, shape, 0)
    col = lax.broadcasted_iota(jnp.int32, shape, 1)
    rel = jnp.abs(col - half - (row % rows_per_head))
    head = row // rows_per_head
    slope = jnp.full(shape, slopes[0], _F32)
    for h in range(1, n_heads):
        slope = jnp.where(head == h, slopes[h], slope)
    dist = (rel * dist_scale).astype(_F32)
    ok = (rel <= half) & (col >= key_lo) & (col < key_hi)
    return jnp.where(ok, -slope * LOG2_E * dist, MASK_VALUE)


def _init_bias_tables(bias_ref, rows_per_head, n_heads, half, slopes, dist_scale):
    bk = rows_per_head + 2 * half
    for t, (lo, hi) in enumerate(((0, bk), (half, bk), (0, rows_per_head + half))):
        bias_ref[t] = _band_bias(rows_per_head, n_heads, bk, half, slopes, dist_scale, lo, hi)


def _table_index(j, n_sub, step, n_steps):
    if j == 0:
        return jnp.where(step == 0, 1, 0)
    if j == n_sub - 1:
        return jnp.where(step == n_steps - 1, 2, 0)
    return 0


def _attn_a_kernel(q_ref, kp_ref, ko_ref, kn_ref, vp_ref, vo_ref, vn_ref, o_ref, lse_ref, bias_ref,
                   *, dilation, slopes, n_steps):
    step = pl.program_id(2)
    bq = ATTN_BQ
    bk = bq + 2 * A_HALF
    planes, rows, _ = q_ref.shape
    n_sub = rows // bq

    @pl.when((pl.program_id(0) == 0) & (pl.program_id(1) == 0) & (step == 0))
    def _():
        _init_bias_tables(bias_ref, bq, A_SLOTS, A_HALF, slopes, dilation)

    first_of_pair = lax.broadcasted_iota(jnp.int32, (bq, LANES), 1) < HEAD_DIM
    for r in range(planes):
        k_all = jnp.concatenate([kp_ref[r], ko_ref[r], kn_ref[r]], axis=0)
        v_all = jnp.concatenate([vp_ref[r], vo_ref[r], vn_ref[r]], axis=0)
        for j in range(n_sub):
            zero = jnp.zeros((bq, LANES), _BF16)
            blocks = []
            for h in range(A_SLOTS):
                q_half = q_ref[r, j * bq:(j + 1) * bq, (h // 2) * LANES:(h // 2 + 1) * LANES]
                own = jnp.where(first_of_pair == (h % 2 == 0), q_half, zero)
                blocks.append(jnp.concatenate([own, zero] if h < 2 else [zero, own], axis=1))
            qs = jnp.concatenate(blocks, axis=0)
            k = k_all[j * bq:j * bq + bk]
            v = v_all[j * bq:j * bq + bk]
            s = lax.dot_general(qs, k, (((1,), (1,)), ((), ())), preferred_element_type=_F32)
            s = s + bias_ref[_table_index(j, n_sub, step, n_steps)]
            m = jnp.max(s, axis=1, keepdims=True)
            p = jnp.exp2(s - m)
            l = jnp.sum(p, axis=1, keepdims=True)
            pv = jnp.dot(p.astype(_BF16), v, preferred_element_type=_F32)
            inv_l = 1.0 / l
            lse2 = m + jnp.log2(l)
            o_halves, lse_halves = [], []
            for pair in range(A_SLOTS // 2):
                lanes = slice(pair * LANES, (pair + 1) * LANES)
                r0 = slice(2 * pair * bq, (2 * pair + 1) * bq)
                r1 = slice((2 * pair + 1) * bq, (2 * pair + 2) * bq)
                o_halves.append(jnp.where(first_of_pair, pv[r0, lanes] * inv_l[r0], pv[r1, lanes] * inv_l[r1]))
                lse_halves.append(jnp.where(first_of_pair, jnp.broadcast_to(lse2[r0], (bq, LANES)),
                                            jnp.broadcast_to(lse2[r1], (bq, LANES))))
            o_ref[r, j * bq:(j + 1) * bq, :] = jnp.concatenate(o_halves, axis=1).astype(o_ref.dtype)
            lse_ref[r, j * bq:(j + 1) * bq, :] = jnp.concatenate(lse_halves, axis=1)


def _init_bias_tables_b(bias_ref, slopes):
    bq, half = ATTN_BQ, B_HALF_WINDOW
    bk = bq + 2 * half
    shape = (B_GROUP * bq, B_KV_HEADS * bk)
    row = lax.broadcasted_iota(jnp.int32, shape, 0)
    col = lax.broadcasted_iota(jnp.int32, shape, 1)
    key = col % bk
    head = (col // bk) * B_GROUP + row // bq
    rel = jnp.abs(key - half - (row % bq))
    slope = jnp.full(shape, slopes[0], _F32)
    for h in range(1, B_Q_HEADS):
        slope = jnp.where(head == h, slopes[h], slope)
    bias = -slope * LOG2_E * rel.astype(_F32)
    for t, (lo, hi) in enumerate(((0, bk), (half, bk), (0, bq + half))):
        bias_ref[t] = jnp.where((rel <= half) & (key >= lo) & (key < hi), bias, MASK_VALUE)


def _attn_b_kernel(q_ref, kp_ref, ko_ref, kn_ref, vp_ref, vo_ref, vn_ref, sink_ref, o_ref, bias_ref,
                   *, slopes, n_steps):
    step = pl.program_id(1)
    bq = ATTN_BQ
    bk = bq + 2 * B_HALF_WINDOW
    n_sub = q_ref.shape[0] // bq

    @pl.when((pl.program_id(0) == 0) & (step == 0))
    def _():
        _init_bias_tables_b(bias_ref, slopes)

    k_all = jnp.concatenate([kp_ref[...], ko_ref[...], kn_ref[...]], axis=0)
    v_all = jnp.concatenate([vp_ref[...], vo_ref[...], vn_ref[...]], axis=0)
    lane_kv = lax.broadcasted_iota(jnp.int32, (bq, B_KV), 1) // HEAD_DIM
    key_lane_kv = lax.broadcasted_iota(jnp.int32, k_all.shape, 1) // HEAD_DIM
    zero = jnp.zeros_like(k_all)
    k_kv = [jnp.where(key_lane_kv == kv, k_all, zero) for kv in range(B_KV_HEADS)]
    v_kv = [jnp.concatenate([jnp.where(key_lane_kv == kv, v_all, zero),
                             jnp.where(key_lane_kv == kv, 1.0, 0.0).astype(_BF16)], axis=1)
            for kv in range(B_KV_HEADS)]
    for j in range(n_sub):
        keys = slice(j * bq, j * bq + bk)
        qs = jnp.concatenate([q_ref[j * bq:(j + 1) * bq, c * B_KV:(c + 1) * B_KV]
                              for c in range(B_GROUP)], axis=0)
        k = jnp.concatenate([k_kv[kv][keys] for kv in range(B_KV_HEADS)], axis=0)
        v = jnp.concatenate([v_kv[kv][keys] for kv in range(B_KV_HEADS)], axis=0)
        s = lax.dot_general(qs, k, (((1,), (1,)), ((), ())), preferred_element_type=_F32)
        s = s + bias_ref[_table_index(j, n_sub, step, n_steps)]
        ps, sink_terms = [], []
        for c in range(B_GROUP):
            p_c, e_c = [], []
            for kv in range(B_KV_HEADS):
                sink = sink_ref[kv * B_GROUP + c]
                s_h = s[c * bq:(c + 1) * bq, kv * bk:(kv + 1) * bk]
                m_h = jnp.maximum(jnp.max(s_h, axis=1, keepdims=True), sink)
                p_c.append(jnp.exp2(s_h - m_h).astype(_BF16))
                e_c.append(jnp.exp2(sink - m_h))
            ps.append(jnp.concatenate(p_c, axis=1))
            sink_terms.append(e_c)
        pv = jnp.dot(jnp.concatenate(ps, axis=0), v, preferred_element_type=_F32)
        for c in range(B_GROUP):
            rows = pv[c * bq:(c + 1) * bq]
            den = rows[:, B_KV:] + jnp.where(lane_kv == 0, sink_terms[c][0], sink_terms[c][1])
            o_ref[j * bq:(j + 1) * bq, c * B_KV:(c + 1) * B_KV] = (rows[:, :B_KV] / den).astype(o_ref.dtype)


def _natural_order(ref, dilation, slab_ref, stage_ref, base):
    n_slab = A_OUT // LANES
    if dilation == 1:
        x = ref[0].astype(_F32)
        return [x[:, s * LANES:(s + 1) * LANES] for s in range(n_slab)]
    rows = ref.shape[1]
    if dilation <= STRIDE_FAST:
        for r in range(dilation):
            blk = ref[r].astype(_F32)
            for s in range(n_slab):
                slab_ref[base + s, pl.ds(r, rows, stride=dilation), :] = blk[:, s * LANES:(s + 1) * LANES]
        return [slab_ref[base + s] for s in range(n_slab)]
    outer = dilation // STRIDE_FAST
    for r in range(dilation):
        r1, r0 = divmod(r, STRIDE_FAST)
        blk = ref[r].astype(_F32)
        for s in range(n_slab):
            stage_ref[(base + s) * STRIDE_FAST + r0, pl.ds(r1, rows, stride=outer), :] = (
                blk[:, s * LANES:(s + 1) * LANES])
    for s in range(n_slab):
        for r0 in range(STRIDE_FAST):
            slab_ref[base + s, pl.ds(r0, rows * outer, stride=STRIDE_FAST), :] = (
                stage_ref[(base + s) * STRIDE_FAST + r0])
    return [slab_ref[base + s] for s in range(n_slab)]


def _mix_ffn_kernel(h_ref, u_ref, oa0_ref, la0_ref, oa1_ref, la1_ref, oa2_ref, la2_ref, yb_ref,
                    wgate_ref, bgate_ref, wpa_ref, wpb_ref, wout_ref,
                    g2_ref, wg_ref, wu_ref, wd_ref, gfin_ref, y_ref, acc_ref, slab_ref, stage_ref):
    n_slab = A_OUT // LANES
    outs, lses = [], []
    for g, (o_ref, l_ref) in enumerate(((oa0_ref, la0_ref), (oa1_ref, la1_ref), (oa2_ref, la2_ref))):
        d = A_PAIRS[g][1]
        outs.append(_natural_order(o_ref, d, slab_ref, stage_ref, 2 * g * n_slab))
        lses.append(_natural_order(l_ref, d, slab_ref, stage_ref, (2 * g + 1) * n_slab))
    ya = []
    for s in range(n_slab):
        la0, la1, la2 = lses[0][s], lses[1][s], lses[2][s]
        mx = jnp.maximum(jnp.maximum(la0, la1), la2)
        e0, e1, e2 = jnp.exp2(la0 - mx), jnp.exp2(la1 - mx), jnp.exp2(la2 - mx)
        ya.append((e0 * outs[0][s] + e1 * outs[1][s] + e2 * outs[2][s]) / (e0 + e1 + e2))
    ya = jnp.concatenate(ya, axis=1)
    h = h_ref[...]
    u = u_ref[...]
    t = jnp.tanh(jnp.dot(u, wgate_ref[...], preferred_element_type=_F32) + bgate_ref[...])
    pa = jnp.dot(ya.astype(_BF16), wpa_ref[...], preferred_element_type=_F32)
    pb = jnp.dot(yb_ref[...], wpb_ref[...], preferred_element_type=_F32)
    mixed = (pa + pb) + t[:, :D_MODEL] * pa + t[:, D_MODEL:] * pb
    h = h + jnp.dot(mixed.astype(_BF16), wout_ref[...], preferred_element_type=_F32)
    hn = _rms(h, g2_ref[...]).astype(_BF16)
    _swiglu_into(acc_ref, hn, wg_ref, wu_ref, wd_ref)
    h = h + 0.5 * acc_ref[...]
    y_ref[...] = _rms(h, gfin_ref[...])


def _resident(shape):
    nd = len(shape)
    return pl.BlockSpec(shape, lambda *_: (0,) * nd, pipeline_mode=pl.Buffered(1))


def _residue_spec(tm, d, width):
    return pl.BlockSpec((None, d, tm // d, width), lambda b, i: (b, 0, i, 0))


def _ffn_proj(x, p):
    batch, seq, _ = x.shape
    tm = TOKEN_TILE
    row = lambda w: pl.BlockSpec((None, tm, w), lambda b, i: (b, i, 0))
    consts = (p["g1"], p["wg1"], p["wu1"], p["wd1"], p["gmix"], p["win"], p["same_head"], p["gain"])
    dils = [d for _, d in A_PAIRS]
    return pl.pallas_call(
        _ffn_proj_kernel,
        grid=(batch, seq // tm),
        in_specs=[row(D_MODEL)] + [_resident(c.shape) for c in consts],
        out_specs=[row(D_MODEL)] * 2 + [_residue_spec(tm, d, GROUP_W) for d in dils] + [row(QKVB_W)],
        out_shape=[jax.ShapeDtypeStruct((batch, seq, D_MODEL), _F32),
                   jax.ShapeDtypeStruct((batch, seq, D_MODEL), _BF16)]
        + [jax.ShapeDtypeStruct((batch, d, seq // d, GROUP_W), _BF16) for d in dils]
        + [jax.ShapeDtypeStruct((batch, seq, QKVB_W), _BF16)],
        scratch_shapes=[pltpu.VMEM((tm, D_MODEL), _F32),
                        pltpu.VMEM((2 * GROUP_W // LANES, tm, LANES), _F32)],
        compiler_params=pltpu.CompilerParams(
            dimension_semantics=("arbitrary",) * 2, vmem_limit_bytes=V7X_VMEM_LIMIT_BYTES),
        name="ffn_proj",
    )(x, *consts)


def _attn_a(qkv, group):
    batch, d, length, _ = qkv.shape
    rows = min(length, ATTN_STEP_ROWS)
    planes = ATTN_STEP_ROWS // rows
    n_steps = length // rows
    halo = rows // A_HALF
    slopes = [float(2.0 ** (-8.0 * (group * A_SLOTS + h + 1) / A_HEADS)) for h in range(A_SLOTS)]

    def own(col):
        return pl.BlockSpec((None, planes, rows, A_OUT), lambda b, r, n: (b, r, n, col))

    def prev(col):
        return pl.BlockSpec((None, planes, A_HALF, A_OUT),
                            lambda b, r, n: (b, r, jnp.maximum(n * halo - 1, 0), col))

    def nxt(col):
        return pl.BlockSpec((None, planes, A_HALF, A_OUT),
                            lambda b, r, n: (b, r, jnp.minimum((n + 1) * halo, n_steps * halo - 1), col))

    return pl.pallas_call(
        functools.partial(_attn_a_kernel, dilation=d, slopes=slopes, n_steps=n_steps),
        grid=(batch, d // planes, n_steps),
        in_specs=[own(0), prev(1), own(1), nxt(1), prev(2), own(2), nxt(2)],
        out_specs=[own(0), own(0)],
        out_shape=[jax.ShapeDtypeStruct((batch, d, length, A_OUT), _BF16),
                   jax.ShapeDtypeStruct((batch, d, length, A_OUT), _F32)],
        scratch_shapes=[pltpu.VMEM((3, A_SLOTS * ATTN_BQ, ATTN_BQ + 2 * A_HALF), _F32)],
        compiler_params=pltpu.CompilerParams(dimension_semantics=("arbitrary",) * 3),
        name=f"attn_a{group}",
    )(qkv, qkv, qkv, qkv, qkv, qkv, qkv)


def _attn_b(qkvb, sink):
    batch, seq, _ = qkvb.shape
    rows = ATTN_STEP_ROWS
    n_steps = seq // rows
    halo = rows // B_HALF_WINDOW
    slopes = [float(2.0 ** (-8.0 * (h + 1) / B_Q_HEADS)) for h in range(B_Q_HEADS)]
    k_col = B_OUT // B_KV

    def own(col):
        return pl.BlockSpec((None, rows, B_KV), lambda b, n: (b, n, col))

    def prev(col):
        return pl.BlockSpec((None, B_HALF_WINDOW, B_KV),
                            lambda b, n: (b, jnp.maximum(n * halo - 1, 0), col))

    def nxt(col):
        return pl.BlockSpec((None, B_HALF_WINDOW, B_KV),
                            lambda b, n: (b, jnp.minimum((n + 1) * halo, n_steps * halo - 1), col))

    q_spec = pl.BlockSpec((None, rows, B_OUT), lambda b, n: (b, n, 0))
    return pl.pallas_call(
        functools.partial(_attn_b_kernel, slopes=slopes, n_steps=n_steps),
        grid=(batch, n_steps),
        in_specs=[q_spec, prev(k_col), own(k_col), nxt(k_col),
                  prev(k_col + 1), own(k_col + 1), nxt(k_col + 1),
                  pl.BlockSpec(memory_space=pltpu.SMEM)],
        out_specs=q_spec,
        out_shape=jax.ShapeDtypeStruct((batch, seq, B_OUT), _BF16),
        scratch_shapes=[pltpu.VMEM((3, B_GROUP * ATTN_BQ, B_KV_HEADS * (ATTN_BQ + 2 * B_HALF_WINDOW)), _F32)],
        compiler_params=pltpu.CompilerParams(dimension_semantics=("arbitrary",) * 2),
        name="attn_b",
    )(qkvb, qkvb, qkvb, qkvb, qkvb, qkvb, qkvb, sink)


def _mix_ffn(h, u, oa, la, yb, p):
    batch, seq, _ = h.shape
    tm = TOKEN_TILE
    row = lambda w: pl.BlockSpec((None, tm, w), lambda b, i: (b, i, 0))
    consts = (p["wgate"], p["bgate"], p["wpa"], p["wpb"], p["wout"],
              p["g2"], p["wg2"], p["wu2"], p["wd2"], p["gfin"])
    attn_specs, attn_args = [], []
    for g, (_, d) in enumerate(A_PAIRS):
        attn_specs += [_residue_spec(tm, d, A_OUT)] * 2
        attn_args += [oa[g], la[g]]
    return pl.pallas_call(
        _mix_ffn_kernel,
        grid=(batch, seq // tm),
        in_specs=[row(D_MODEL)] * 2 + attn_specs + [row(B_OUT)] + [_resident(c.shape) for c in consts],
        out_specs=row(D_MODEL),
        out_shape=jax.ShapeDtypeStruct((batch, seq, D_MODEL), _F32),
        scratch_shapes=[pltpu.VMEM((tm, D_MODEL), _F32),
                        pltpu.VMEM((2 * N_GROUPS * A_OUT // LANES, tm, LANES), _F32),
                        pltpu.VMEM((2 * N_GROUPS * A_OUT // LANES * STRIDE_FAST, tm // STRIDE_FAST, LANES), _F32)],
        compiler_params=pltpu.CompilerParams(
            dimension_semantics=("arbitrary",) * 2, vmem_limit_bytes=V7X_VMEM_LIMIT_BYTES),
        name="mix_ffn",
    )(h, u, *attn_args, yb, *consts)


def _b_head_perm():
    idx = [(kv * B_GROUP + c) * HEAD_DIM + e
           for c in range(B_GROUP) for kv in range(B_KV_HEADS) for e in range(HEAD_DIM)]
    return np.asarray(idx, dtype=np.int32)


def _prepare_params(ffn1_norm, ffn1_w_gate, ffn1_w_up, ffn1_w_down, mix_norm, w_in, a_q_norm,
                    a_k_norm, b_q_norm, b_k_norm, b_sink, w_proj_a, w_proj_b, w_gate, b_gate, w_out,
                    ffn2_norm, ffn2_w_gate, ffn2_w_up, ffn2_w_down, final_norm):
    def vec(g):
        return g.reshape(1, -1).astype(_F32)

    def bf16(w):
        return w.astype(_BF16)

    def half(w):
        return (0.5 * w).astype(_BF16)

    perm = _b_head_perm()
    qa, ka, va, qb, kb, vb = jnp.split(w_in, np.cumsum([A_W, A_W, A_W, B_OUT, B_KV]).tolist(), axis=1)
    grp = lambda w, g: w[:, g * A_OUT:(g + 1) * A_OUT]
    chunks = [m for g in range(N_GROUPS) for m in (grp(qa, g), grp(ka, g), grp(va, g))] + [qb[:, perm], kb, vb]
    win = jnp.concatenate(chunks, axis=1).astype(_BF16)
    scale = HEAD_DIM ** -0.5 * LOG2_E
    ones = lambda n: jnp.ones((n,), _F32)
    gain = jnp.concatenate(
        [jnp.tile(a_q_norm * scale, A_SLOTS), jnp.tile(a_k_norm, A_SLOTS), ones(A_OUT)] * N_GROUPS
        + [jnp.tile(b_q_norm * scale, B_Q_HEADS), jnp.tile(b_k_norm, B_KV_HEADS), ones(B_KV)])
    head_of_col = np.arange(MXU_TILE) // HEAD_DIM
    same_head = jnp.asarray(head_of_col[:, None] == head_of_col[None, :], dtype=_BF16)
    return dict(
        g1=vec(ffn1_norm), wg1=half(ffn1_w_gate), wu1=bf16(ffn1_w_up), wd1=bf16(ffn1_w_down),
        gmix=vec(mix_norm), win=win, same_head=same_head, gain=vec(gain), sink=b_sink.astype(_F32) * LOG2_E,
        wgate=half(w_gate), bgate=vec(0.5 * b_gate), wpa=w_proj_a.astype(_BF16),
        wpb=w_proj_b[perm, :].astype(_BF16), wout=half(w_out),
        g2=vec(ffn2_norm), wg2=half(ffn2_w_gate), wu2=bf16(ffn2_w_up), wd2=bf16(ffn2_w_down),
        gfin=vec(final_norm),
    )


def _encoder(x, p):
    h, u, a0, a1, a2, qkvb = _ffn_proj(x, p)
    oa, la = zip(*[_attn_a(a, g) for g, a in enumerate((a0, a1, a2))])
    yb = _attn_b(qkvb, p["sink"])
    return _mix_ffn(h, u, oa, la, yb, p)


def kernel(x_prompt, x_sample, ffn1_norm, ffn1_w_gate, ffn1_w_up, ffn1_w_down, mix_norm, w_in, a_q_norm, a_k_norm, b_q_norm, b_k_norm, b_sink, w_proj_a, w_proj_b, w_gate, b_gate, w_out, ffn2_norm, ffn2_w_gate, ffn2_w_up, ffn2_w_down, final_norm):
    p = _prepare_params(*[w[0] for w in (
        ffn1_norm, ffn1_w_gate, ffn1_w_up, ffn1_w_down, mix_norm, w_in, a_q_norm, a_k_norm, b_q_norm,
        b_k_norm, b_sink, w_proj_a, w_proj_b, w_gate, b_gate, w_out, ffn2_norm, ffn2_w_gate, ffn2_w_up,
        ffn2_w_down, final_norm)])
    return (_encoder(x_prompt, p), _encoder(x_sample, p))
```

```python
import functools

import numpy as np
import jax
import jax.numpy as jnp
from jax import lax
from jax.experimental import pallas as pl
from jax.experimental.pallas import tpu as pltpu

D_MODEL = 1024
D_FF = 2816
HEAD_DIM = 64
A_PAIRS = ((128, 1), (512, 4), (2048, 16))
N_GROUPS = len(A_PAIRS)
A_SLOTS = 4
A_HEADS = A_SLOTS * N_GROUPS
A_OUT = A_SLOTS * HEAD_DIM
A_W = A_HEADS * HEAD_DIM
B_Q_HEADS = 8
B_KV_HEADS = 2
B_GROUP = B_Q_HEADS // B_KV_HEADS
B_HALF_WINDOW = 128
B_OUT = B_Q_HEADS * HEAD_DIM
B_KV = B_KV_HEADS * HEAD_DIM
NORM_EPS = 1e-6
MASK_VALUE = -1e30
LOG2_E = 1.4426950408889634

MXU_TILE = 256
GROUP_W = 3 * A_OUT
QKVB_W = B_OUT + 2 * B_KV

LANES = 128
STRIDE_FAST = 4
FF_CHUNKS = (512,) * 5 + (256,)
TOKEN_TILE = 512
ATTN_BQ = 128
ATTN_STEP_ROWS = 4096
A_HALF = 64
V7X_VMEM_LIMIT_BYTES = 56 * 1024 * 1024

_F32 = jnp.float32
_BF16 = jnp.bfloat16


def _rms(x, g):
    ms = jnp.mean(x * x, axis=-1, keepdims=True)
    return x * lax.rsqrt(ms + NORM_EPS) * g


def _swiglu_into(acc_ref, xn, wg_half_ref, wu_ref, wd_ref):
    lo = 0
    for w in FF_CHUNKS:
        g = jnp.dot(xn, wg_half_ref[:, lo:lo + w], preferred_element_type=_F32)
        u = jnp.dot(xn, wu_ref[:, lo:lo + w], preferred_element_type=_F32)
        a = (g * (1.0 + jnp.tanh(g)) * u).astype(_BF16)
        part = jnp.dot(a, wd_ref[lo:lo + w, :], preferred_element_type=_F32)
        if lo == 0:
            acc_ref[...] = part
        else:
            acc_ref[...] += part
        lo += w


def _qk_normed(chunk, n_norm, same_head_ref, gain):
    parts = []
    for lo in range(0, n_norm, MXU_TILE):
        w = min(MXU_TILE, n_norm - lo)
        x = chunk[:, lo:lo + w]
        ssq = jnp.dot((x * x).astype(_BF16), same_head_ref[:w, :w], preferred_element_type=_F32)
        parts.append(x * lax.rsqrt(ssq * (1.0 / HEAD_DIM) + NORM_EPS))
    return jnp.concatenate(parts, axis=1) * gain, chunk[:, n_norm:]


def _ffn_proj_kernel(x_ref, g1_ref, wg_ref, wu_ref, wd_ref, gmix_ref, win_ref, same_head_ref,
                     gain_ref, h_ref, u_ref, a0_ref, a1_ref, a2_ref, qkvb_ref, acc_ref, slab_ref, stage_ref):
    tm = x_ref.shape[0]
    x = x_ref[...]
    xn = _rms(x, g1_ref[...]).astype(_BF16)
    _swiglu_into(acc_ref, xn, wg_ref, wu_ref, wd_ref)
    h = x + 0.5 * acc_ref[...]
    h_ref[...] = h
    u = _rms(h, gmix_ref[...]).astype(_BF16)
    u_ref[...] = u
    n_slab = GROUP_W // LANES

    def project_group(g, a_ref):
        lo = g * GROUP_W
        chunk = jnp.dot(u, win_ref[:, lo:lo + GROUP_W], preferred_element_type=_F32)
        nq, val = _qk_normed(chunk, 2 * A_OUT, same_head_ref, gain_ref[:, lo:lo + 2 * A_OUT])
        d = A_PAIRS[g][1]
        if d == 1:
            a_ref[0, :, :2 * A_OUT] = nq.astype(_BF16)
            a_ref[0, :, 2 * A_OUT:] = val.astype(_BF16)
            return
        base = (g - 1) * n_slab
        for s in range(n_slab):
            src = nq if s * LANES < 2 * A_OUT else val
            col = s * LANES % (2 * A_OUT)
            slab_ref[base + s] = src[:, col:col + LANES]
        if d <= STRIDE_FAST:
            for r in range(d):
                for s in range(n_slab):
                    a_ref[r, :, s * LANES:(s + 1) * LANES] = (
                        slab_ref[base + s, pl.ds(r, tm // d, stride=d), :].astype(_BF16))
            return
        outer = d // STRIDE_FAST
        for s in range(n_slab):
            for r0 in range(STRIDE_FAST):
                stage_ref[s * STRIDE_FAST + r0] = slab_ref[base + s, pl.ds(r0, tm // STRIDE_FAST, stride=STRIDE_FAST), :]
        for r in range(d):
            r1, r0 = divmod(r, STRIDE_FAST)
            for s in range(n_slab):
                a_ref[r, :, s * LANES:(s + 1) * LANES] = (
                    stage_ref[s * STRIDE_FAST + r0, pl.ds(r1, tm // d, stride=outer), :].astype(_BF16))

    def project_b():
        lo = N_GROUPS * GROUP_W
        chunk = jnp.dot(u, win_ref[:, lo:lo + QKVB_W], preferred_element_type=_F32)
        nq, val = _qk_normed(chunk, B_OUT + B_KV, same_head_ref, gain_ref[:, lo:lo + B_OUT + B_KV])
        qkvb_ref[:, :B_OUT + B_KV] = nq.astype(_BF16)
        qkvb_ref[:, B_OUT + B_KV:] = val.astype(_BF16)

    project_group(2, a2_ref)
    project_group(1, a1_ref)
    project_group(0, a0_ref)
    project_b()


def _band_bias(rows_per_head, n_heads, bk, half, slopes, dist_scale, key_lo, key_hi):
    shape = (n_heads * rows_per_head, bk)
    row = lax.broadcasted_iota(jnp.int32, shape, 0)
    col = lax.broadcasted_iota(jnp.int32, shape, 1)
    rel = jnp.abs(col - half - (row % rows_per_head))
    head = row // rows_per_head
    slope = jnp.full(shape, slopes[0], _F32)
    for h in range(1, n_heads):
        slope = jnp.where(head == h, slopes[h], slope)
    dist = (rel * dist_scale).astype(_F32)
    ok = (rel <= half) & (col >= key_lo) & (col < key_hi)
    return jnp.where(ok, -slope * LOG2_E * dist, MASK_VALUE)


def _init_bias_tables(bias_ref, rows_per_head, n_heads, half, slopes, dist_scale):
    bk = rows_per_head + 2 * half
    for t, (lo, hi) in enumerate(((0, bk), (half, bk), (0, rows_per_head + half))):
        bias_ref[t] = _band_bias(rows_per_head, n_heads, bk, half, slopes, dist_scale, lo, hi)


def _table_index(j, n_sub, step, n_steps):
    if j == 0:
        return jnp.where(step == 0, 1, 0)
    if j == n_sub - 1:
        return jnp.where(step == n_steps - 1, 2, 0)
    return 0


def _attn_a_kernel(q_ref, kp_ref, ko_ref, kn_ref, vp_ref, vo_ref, vn_ref, o_ref, lse_ref, bias_ref,
                   *, dilation, slopes, n_steps):
    step = pl.program_id(2)
    bq = ATTN_BQ
    bk = bq + 2 * A_HALF
    planes, rows, _ = q_ref.shape
    n_sub = rows // bq

    @pl.when((pl.program_id(0) == 0) & (pl.program_id(1) == 0) & (step == 0))
    def _():
        _init_bias_tables(bias_ref, bq, A_SLOTS, A_HALF, slopes, dilation)

    first_of_pair = lax.broadcasted_iota(jnp.int32, (bq, LANES), 1) < HEAD_DIM
    for r in range(planes):
        k_all = jnp.concatenate([kp_ref[r], ko_ref[r], kn_ref[r]], axis=0)
        v_all = jnp.concatenate([vp_ref[r], vo_ref[r], vn_ref[r]], axis=0)
        for j in range(n_sub):
            zero = jnp.zeros((bq, LANES), _BF16)
            blocks = []
            for h in range(A_SLOTS):
                q_half = q_ref[r, j * bq:(j + 1) * bq, (h // 2) * LANES:(h // 2 + 1) * LANES]
                own = jnp.where(first_of_pair == (h % 2 == 0), q_half, zero)
                blocks.append(jnp.concatenate([own, zero] if h < 2 else [zero, own], axis=1))
            qs = jnp.concatenate(blocks, axis=0)
            k = k_all[j * bq:j * bq + bk]
            v = v_all[j * bq:j * bq + bk]
            s = lax.dot_general(qs, k, (((1,), (1,)), ((), ())), preferred_element_type=_F32)
            s = s + bias_ref[_table_index(j, n_sub, step, n_steps)]
            m = jnp.max(s, axis=1, keepdims=True)
            p = jnp.exp2(s - m)
            l = jnp.sum(p, axis=1, keepdims=True)
            pv = jnp.dot(p.astype(_BF16), v, preferred_element_type=_F32)
            inv_l = 1.0 / l
            lse2 = m + jnp.log2(l)
            o_halves, lse_halves = [], []
            for pair in range(A_SLOTS // 2):
                lanes = slice(pair * LANES, (pair + 1) * LANES)
                r0 = slice(2 * pair * bq, (2 * pair + 1) * bq)
                r1 = slice((2 * pair + 1) * bq, (2 * pair + 2) * bq)
                o_halves.append(jnp.where(first_of_pair, pv[r0, lanes] * inv_l[r0], pv[r1, lanes] * inv_l[r1]))
                lse_halves.append(jnp.where(first_of_pair, jnp.broadcast_to(lse2[r0], (bq, LANES)),
                                            jnp.broadcast_to(lse2[r1], (bq, LANES))))
            o_ref[r, j * bq:(j + 1) * bq, :] = jnp.concatenate(o_halves, axis=1).astype(o_ref.dtype)
            lse_ref[r, j * bq:(j + 1) * bq, :] = jnp.concatenate(lse_halves, axis=1)


def _init_bias_tables_b(bias_ref, slopes):
    bq, half = ATTN_BQ, B_HALF_WINDOW
    bk = bq + 2 * half
    shape = (B_GROUP * bq, B_KV_HEADS * bk)
    row = lax.broadcasted_iota(jnp.int32, shape, 0)
    col = lax.broadcasted_iota(jnp.int32, shape, 1)
    key = col % bk
    head = (col // bk) * B_GROUP + row // bq
    rel = jnp.abs(key - half - (row % bq))
    slope = jnp.full(shape, slopes[0], _F32)
    for h in range(1, B_Q_HEADS):
        slope = jnp.where(head == h, slopes[h], slope)
    bias = -slope * LOG2_E * rel.astype(_F32)
    for t, (lo, hi) in enumerate(((0, bk), (half, bk), (0, bq + half))):
        bias_ref[t] = jnp.where((rel <= half) & (key >= lo) & (key < hi), bias, MASK_VALUE)


def _attn_b_kernel(q_ref, kp_ref, ko_ref, kn_ref, vp_ref, vo_ref, vn_ref, sink_ref, o_ref, bias_ref,
                   *, slopes, n_steps):
    step = pl.program_id(1)
    bq = ATTN_BQ
    bk = bq + 2 * B_HALF_WINDOW
    n_sub = q_ref.shape[0] // bq

    @pl.when((pl.program_id(0) == 0) & (step == 0))
    def _():
        _init_bias_tables_b(bias_ref, slopes)

    k_all = jnp.concatenate([kp_ref[...], ko_ref[...], kn_ref[...]], axis=0)
    v_all = jnp.concatenate([vp_ref[...], vo_ref[...], vn_ref[...]], axis=0)
    lane_kv = lax.broadcasted_iota(jnp.int32, (bq, B_KV), 1) // HEAD_DIM
    key_lane_kv = lax.broadcasted_iota(jnp.int32, k_all.shape, 1) // HEAD_DIM
    zero = jnp.zeros_like(k_all)
    k_kv = [jnp.where(key_lane_kv == kv, k_all, zero) for kv in range(B_KV_HEADS)]
    v_kv = [jnp.concatenate([jnp.where(key_lane_kv == kv, v_all, zero),
                             jnp.where(key_lane_kv == kv, 1.0, 0.0).astype(_BF16)], axis=1)
            for kv in range(B_KV_HEADS)]
    for j in range(n_sub):
        keys = slice(j * bq, j * bq + bk)
        qs = jnp.concatenate([q_ref[j * bq:(j + 1) * bq, c * B_KV:(c + 1) * B_KV]
                              for c in range(B_GROUP)], axis=0)
        k = jnp.concatenate([k_kv[kv][keys] for kv in range(B_KV_HEADS)], axis=0)
        v = jnp.concatenate([v_kv[kv][keys] for kv in range(B_KV_HEADS)], axis=0)
        s = lax.dot_general(qs, k, (((1,), (1,)), ((), ())), preferred_element_type=_F32)
        s = s + bias_ref[_table_index(j, n_sub, step, n_steps)]
        ps, sink_terms = [], []
        for c in range(B_GROUP):
            p_c, e_c = [], []
            for kv in range(B_KV_HEADS):
                sink = sink_ref[kv * B_GROUP + c]
                s_h = s[c * bq:(c + 1) * bq, kv * bk:(kv + 1) * bk]
                m_h = jnp.maximum(jnp.max(s_h, axis=1, keepdims=True), sink)
                p_c.append(jnp.exp2(s_h - m_h).astype(_BF16))
                e_c.append(jnp.exp2(sink - m_h))
            ps.append(jnp.concatenate(p_c, axis=1))
            sink_terms.append(e_c)
        pv = jnp.dot(jnp.concatenate(ps, axis=0), v, preferred_element_type=_F32)
        for c in range(B_GROUP):
            rows = pv[c * bq:(c + 1) * bq]
            den = rows[:, B_KV:] + jnp.where(lane_kv == 0, sink_terms[c][0], sink_terms[c][1])
            o_ref[j * bq:(j + 1) * bq, c * B_KV:(c + 1) * B_KV] = (rows[:, :B_KV] / den).astype(o_ref.dtype)


def _natural_order(ref, dilation, slab_ref, stage_ref, base):
    n_slab = A_OUT // LANES
    if dilation == 1:
        x = ref[0].astype(_F32)
        return [x[:, s * LANES:(s + 1) * LANES] for s in range(n_slab)]
    rows = ref.shape[1]
    if dilation <= STRIDE_FAST:
        for r in range(dilation):
            blk = ref[r].astype(_F32)
            for s in range(n_slab):
                slab_ref[base + s, pl.ds(r, rows, stride=dilation), :] = blk[:, s * LANES:(s + 1) * LANES]
        return [slab_ref[base + s] for s in range(n_slab)]
    outer = dilation // STRIDE_FAST
    for r in range(dilation):
        r1, r0 = divmod(r, STRIDE_FAST)
        blk = ref[r].astype(_F32)
        for s in range(n_slab):
            stage_ref[(base + s) * STRIDE_FAST + r0, pl.ds(r1, rows, stride=outer), :] = (
                blk[:, s * LANES:(s + 1) * LANES])
    for s in range(n_slab):
        for r0 in range(STRIDE_FAST):
            slab_ref[base + s, pl.ds(r0, rows * outer, stride=STRIDE_FAST), :] = (
                stage_ref[(base + s) * STRIDE_FAST + r0])
    return [slab_ref[base + s] for s in range(n_slab)]


def _mix_ffn_kernel(h_ref, u_ref, oa0_ref, la0_ref, oa1_ref, la1_ref, oa2_ref, la2_ref, yb_ref,
                    wgate_ref, bgate_ref, wpa_ref, wpb_ref, wout_ref,
                    g2_ref, wg_ref, wu_ref, wd_ref, gfin_ref, y_ref, acc_ref, slab_ref, stage_ref):
    n_slab = A_OUT // LANES
    outs, lses = [], []
    for g, (o_ref, l_ref) in enumerate(((oa0_ref, la0_ref), (oa1_ref, la1_ref), (oa2_ref, la2_ref))):
        d = A_PAIRS[g][1]
        outs.append(_natural_order(o_ref, d, slab_ref, stage_ref, 2 * g * n_slab))
        lses.append(_natural_order(l_ref, d, slab_ref, stage_ref, (2 * g + 1) * n_slab))
    ya = []
    for s in range(n_slab):
        la0, la1, la2 = lses[0][s], lses[1][s], lses[2][s]
        mx = jnp.maximum(jnp.maximum(la0, la1), la2)
        e0, e1, e2 = jnp.exp2(la0 - mx), jnp.exp2(la1 - mx), jnp.exp2(la2 - mx)
        ya.append((e0 * outs[0][s] + e1 * outs[1][s] + e2 * outs[2][s]) / (e0 + e1 + e2))
    ya = jnp.concatenate(ya, axis=1)
    h = h_ref[...]
    u = u_ref[...]
    t = jnp.tanh(jnp.dot(u, wgate_ref[...], preferred_element_type=_F32) + bgate_ref[...])
    pa = jnp.dot(ya.astype(_BF16), wpa_ref[...], preferred_element_type=_F32)
    pb = jnp.dot(yb_ref[...], wpb_ref[...], preferred_element_type=_F32)
    mixed = (pa + pb) + t[:, :D_MODEL] * pa + t[:, D_MODEL:] * pb
    h = h + jnp.dot(mixed.astype(_BF16), wout_ref[...], preferred_element_type=_F32)
    hn = _rms(h, g2_ref[...]).astype(_BF16)
    _swiglu_into(acc_ref, hn, wg_ref, wu_ref, wd_ref)
    h = h + 0.5 * acc_ref[...]
    y_ref[...] = _rms(h, gfin_ref[...])


def _resident(shape):
    nd = len(shape)
    return pl.BlockSpec(shape, lambda *_: (0,) * nd, pipeline_mode=pl.Buffered(1))


def _residue_spec(tm, d, width):
    return pl.BlockSpec((None, d, tm // d, width), lambda b, i: (b, 0, i, 0))


def _ffn_proj(x, p):
    batch, seq, _ = x.shape
    tm = TOKEN_TILE
    row = lambda w: pl.BlockSpec((None, tm, w), lambda b, i: (b, i, 0))
    consts = (p["g1"], p["wg1"], p["wu1"], p["wd1"], p["gmix"], p["win"], p["same_head"], p["gain"])
    dils = [d for _, d in A_PAIRS]
    return pl.pallas_call(
        _ffn_proj_kernel,
        grid=(batch, seq // tm),
        in_specs=[row(D_MODEL)] + [_resident(c.shape) for c in consts],
        out_specs=[row(D_MODEL)] * 2 + [_residue_spec(tm, d, GROUP_W) for d in dils] + [row(QKVB_W)],
        out_shape=[jax.ShapeDtypeStruct((batch, seq, D_MODEL), _F32),
                   jax.ShapeDtypeStruct((batch, seq, D_MODEL), _BF16)]
        + [jax.ShapeDtypeStruct((batch, d, seq // d, GROUP_W), _BF16) for d in dils]
        + [jax.ShapeDtypeStruct((batch, seq, QKVB_W), _BF16)],
        scratch_shapes=[pltpu.VMEM((tm, D_MODEL), _F32),
                        pltpu.VMEM((2 * GROUP_W // LANES, tm, LANES), _F32),
                        pltpu.VMEM((GROUP_W // LANES * STRIDE_FAST, tm // STRIDE_FAST, LANES), _F32)],
        compiler_params=pltpu.CompilerParams(
            dimension_semantics=("arbitrary",) * 2, vmem_limit_bytes=V7X_VMEM_LIMIT_BYTES),
        name="ffn_proj",
    )(x, *consts)


def _attn_a(qkv, group):
    batch, d, length, _ = qkv.shape
    rows = min(length, ATTN_STEP_ROWS)
    planes = ATTN_STEP_ROWS // rows
    n_steps = length // rows
    halo = rows // A_HALF
    slopes = [float(2.0 ** (-8.0 * (group * A_SLOTS + h + 1) / A_HEADS)) for h in range(A_SLOTS)]

    def own(col):
        return pl.BlockSpec((None, planes, rows, A_OUT), lambda b, r, n: (b, r, n, col))

    def prev(col):
        return pl.BlockSpec((None, planes, A_HALF, A_OUT),
                            lambda b, r, n: (b, r, jnp.maximum(n * halo - 1, 0), col))

    def nxt(col):
        return pl.BlockSpec((None, planes, A_HALF, A_OUT),
                            lambda b, r, n: (b, r, jnp.minimum((n + 1) * halo, n_steps * halo - 1), col))

    return pl.pallas_call(
        functools.partial(_attn_a_kernel, dilation=d, slopes=slopes, n_steps=n_steps),
        grid=(batch, d // planes, n_steps),
        in_specs=[own(0), prev(1), own(1), nxt(1), prev(2), own(2), nxt(2)],
        out_specs=[own(0), own(0)],
        out_shape=[jax.ShapeDtypeStruct((batch, d, length, A_OUT), _BF16),
                   jax.ShapeDtypeStruct((batch, d, length, A_OUT), _F32)],
        scratch_shapes=[pltpu.VMEM((3, A_SLOTS * ATTN_BQ, ATTN_BQ + 2 * A_HALF), _F32)],
        compiler_params=pltpu.CompilerParams(dimension_semantics=("arbitrary",) * 3),
        name=f"attn_a{group}",
    )(qkv, qkv, qkv, qkv, qkv, qkv, qkv)


def _attn_b(qkvb, sink):
    batch, seq, _ = qkvb.shape
    rows = ATTN_STEP_ROWS
    n_steps = seq // rows
    halo = rows // B_HALF_WINDOW
    slopes = [float(2.0 ** (-8.0 * (h + 1) / B_Q_HEADS)) for h in range(B_Q_HEADS)]
    k_col = B_OUT // B_KV

    def own(col):
        return pl.BlockSpec((None, rows, B_KV), lambda b, n: (b, n, col))

    def prev(col):
        return pl.BlockSpec((None, B_HALF_WINDOW, B_KV),
                            lambda b, n: (b, jnp.maximum(n * halo - 1, 0), col))

    def nxt(col):
        return pl.BlockSpec((None, B_HALF_WINDOW, B_KV),
                            lambda b, n: (b, jnp.minimum((n + 1) * halo, n_steps * halo - 1), col))

    q_spec = pl.BlockSpec((None, rows, B_OUT), lambda b, n: (b, n, 0))
    return pl.pallas_call(
        functools.partial(_attn_b_kernel, slopes=slopes, n_steps=n_steps),
        grid=(batch, n_steps),
        in_specs=[q_spec, prev(k_col), own(k_col), nxt(k_col),
                  prev(k_col + 1), own(k_col + 1), nxt(k_col + 1),
                  pl.BlockSpec(memory_space=pltpu.SMEM)],
        out_specs=q_spec,
        out_shape=jax.ShapeDtypeStruct((batch, seq, B_OUT), _BF16),
        scratch_shapes=[pltpu.VMEM((3, B_GROUP * ATTN_BQ, B_KV_HEADS * (ATTN_BQ + 2 * B_HALF_WINDOW)), _F32)],
        compiler_params=pltpu.CompilerParams(dimension_semantics=("arbitrary",) * 2),
        name="attn_b",
    )(qkvb, qkvb, qkvb, qkvb, qkvb, qkvb, qkvb, sink)


def _mix_ffn(h, u, oa, la, yb, p):
    batch, seq, _ = h.shape
    tm = TOKEN_TILE
    row = lambda w: pl.BlockSpec((None, tm, w), lambda b, i: (b, i, 0))
    consts = (p["wgate"], p["bgate"], p["wpa"], p["wpb"], p["wout"],
              p["g2"], p["wg2"], p["wu2"], p["wd2"], p["gfin"])
    attn_specs, attn_args = [], []
    for g, (_, d) in enumerate(A_PAIRS):
        attn_specs += [_residue_spec(tm, d, A_OUT)] * 2
        attn_args += [oa[g], la[g]]
    return pl.pallas_call(
        _mix_ffn_kernel,
        grid=(batch, seq // tm),
        in_specs=[row(D_MODEL)] * 2 + attn_specs + [row(B_OUT)] + [_resident(c.shape) for c in consts],
        out_specs=row(D_MODEL),
        out_shape=jax.ShapeDtypeStruct((batch, seq, D_MODEL), _F32),
        scratch_shapes=[pltpu.VMEM((tm, D_MODEL), _F32),
                        pltpu.VMEM((2 * N_GROUPS * A_OUT // LANES, tm, LANES), _F32),
                        pltpu.VMEM((2 * N_GROUPS * A_OUT // LANES * STRIDE_FAST, tm // STRIDE_FAST, LANES), _F32)],
        compiler_params=pltpu.CompilerParams(
            dimension_semantics=("arbitrary",) * 2, vmem_limit_bytes=V7X_VMEM_LIMIT_BYTES),
        name="mix_ffn",
    )(h, u, *attn_args, yb, *consts)


def _b_head_perm():
    idx = [(kv * B_GROUP + c) * HEAD_DIM + e
           for c in range(B_GROUP) for kv in range(B_KV_HEADS) for e in range(HEAD_DIM)]
    return np.asarray(idx, dtype=np.int32)


def _prepare_params(ffn1_norm, ffn1_w_gate, ffn1_w_up, ffn1_w_down, mix_norm, w_in, a_q_norm,
                    a_k_norm, b_q_norm, b_k_norm, b_sink, w_proj_a, w_proj_b, w_gate, b_gate, w_out,
                    ffn2_norm, ffn2_w_gate, ffn2_w_up, ffn2_w_down, final_norm):
    def vec(g):
        return g.reshape(1, -1).astype(_F32)

    def bf16(w):
        return w.astype(_BF16)

    def half(w):
        return (0.5 * w).astype(_BF16)

    perm = _b_head_perm()
    qa, ka, va, qb, kb, vb = jnp.split(w_in, np.cumsum([A_W, A_W, A_W, B_OUT, B_KV]).tolist(), axis=1)
    grp = lambda w, g: w[:, g * A_OUT:(g + 1) * A_OUT]
    chunks = [m for g in range(N_GROUPS) for m in (grp(qa, g), grp(ka, g), grp(va, g))] + [qb[:, perm], kb, vb]
    win = jnp.concatenate(chunks, axis=1).astype(_BF16)
    scale = HEAD_DIM ** -0.5 * LOG2_E
    ones = lambda n: jnp.ones((n,), _F32)
    gain = jnp.concatenate(
        [jnp.tile(a_q_norm * scale, A_SLOTS), jnp.tile(a_k_norm, A_SLOTS), ones(A_OUT)] * N_GROUPS
        + [jnp.tile(b_q_norm * scale, B_Q_HEADS), jnp.tile(b_k_norm, B_KV_HEADS), ones(B_KV)])
    head_of_col = np.arange(MXU_TILE) // HEAD_DIM
    same_head = jnp.asarray(head_of_col[:, None] == head_of_col[None, :], dtype=_BF16)
    return dict(
        g1=vec(ffn1_norm), wg1=half(ffn1_w_gate), wu1=bf16(ffn1_w_up), wd1=bf16(ffn1_w_down),
        gmix=vec(mix_norm), win=win, same_head=same_head, gain=vec(gain), sink=b_sink.astype(_F32) * LOG2_E,
        wgate=half(w_gate), bgate=vec(0.5 * b_gate), wpa=w_proj_a.astype(_BF16),
        wpb=w_proj_b[perm, :].astype(_BF16), wout=half(w_out),
        g2=vec(ffn2_norm), wg2=half(ffn2_w_gate), wu2=bf16(ffn2_w_up), wd2=bf16(ffn2_w_down),
        gfin=vec(final_norm),
    )


def _encoder(x, p):
    h, u, a0, a1, a2, qkvb = _ffn_proj(x, p)
    oa, la = zip(*[_attn_a(a, g) for g, a in enumerate((a0, a1, a2))])
    yb = _attn_b(qkvb, p["sink"])
    return _mix_ffn(h, u, oa, la, yb, p)


def kernel(x_prompt, x_sample, ffn1_norm, ffn1_w_gate, ffn1_w_up, ffn1_w_down, mix_norm, w_in, a_q_norm, a_k_norm, b_q_norm, b_k_norm, b_sink, w_proj_a, w_proj_b, w_gate, b_gate, w_out, ffn2_norm, ffn2_w_gate, ffn2_w_up, ffn2_w_down, final_norm):
    p = _prepare_params(*[w[0] for w in (
        ffn1_norm, ffn1_w_gate, ffn1_w_up, ffn1_w_down, mix_norm, w_in, a_q_norm, a_k_norm, b_q_norm,
        b_k_norm, b_sink, w_proj_a, w_proj_b, w_gate, b_gate, w_out, ffn2_norm, ffn2_w_gate, ffn2_w_up,
        ffn2_w_down, final_norm)])
    return (_encoder(x_prompt, p), _encoder(x_sample, p))
```
